```python
import jax, jax.numpy as jnp
from jax import lax
import numpy as np

D_MODEL = 1024
BATCH = 16
SEQ = 2048
DEPTH = 4

LRU_WIDTH = 512
LRU_BLOCKS = 8
LRU_BLOCK_DIM = LRU_WIDTH // LRU_BLOCKS
LRU_C = 8.0
CONV_WIDTH = 4
GDN_HEADS = 4
GDN_DK = 128
GDN_DV = 128
GDN_CHUNK = 64
SB_HEADS = 8
SB_DH = 64
SB_BLOCK = 128
GLA_HEADS = 4
GLA_DK = 128
GLA_DV = 128
GLA_GATE_RANK = 16
GLA_TAU = 16.0
GLA_CHUNK = 16
N_BRANCHES = 4
BRANCH_WIDTH = 512
N_EXPERTS = 16
N_GROUPS = 4
EXPERTS_PER_GROUP = N_EXPERTS // N_GROUPS
TOP_K = 2
D_EXPERT = 512
LN_EPS = 1e-5
NORM_EPS = 1e-6
DN_ALPHA = (2 * DEPTH) ** 0.25
DN_BETA = (8 * DEPTH) ** -0.25

IN_SPLITS = (
    LRU_WIDTH,
    3 * GDN_HEADS * GDN_DK,
    GDN_HEADS,
    GDN_HEADS,
    GDN_HEADS * GDN_DV,
    3 * SB_HEADS * SB_DH,
    GLA_HEADS * (2 * GLA_DK + GLA_DV),
    GLA_GATE_RANK,
    GLA_HEADS * GLA_DV,
    N_BRANCHES * D_MODEL,
)
D_IN = sum(IN_SPLITS)

kernel_name = "hybrid_rglru_gdn_stickbreak_gla_groupmoe"


def layer_norm(x, g, b):
    xf = x.astype(jnp.float32)
    mu = jnp.mean(xf, -1, keepdims=True)
    var = jnp.mean(jnp.square(xf - mu), -1, keepdims=True)
    y = (xf - mu) * lax.rsqrt(var + LN_EPS) * g.astype(jnp.float32) + b.astype(jnp.float32)
    return y.astype(x.dtype)


def rms_norm(x, g):
    xf = x.astype(jnp.float32)
    return xf * lax.rsqrt(jnp.mean(jnp.square(xf), -1, keepdims=True) + NORM_EPS) * g.astype(jnp.float32)


def l2_normalize(x):
    return x * lax.rsqrt(jnp.sum(jnp.square(x), -1, keepdims=True) + NORM_EPS)


def causal_depthwise_conv(x, w):
    return lax.conv_general_dilated(
        x, w[:, None, :].astype(x.dtype), window_strides=(1,),
        padding=((w.shape[0] - 1, 0),), dimension_numbers=("NWC", "WIO", "NWC"),
        feature_group_count=x.shape[-1])


def _to_chunks(t, c):
    b, s, h = t.shape[:3]
    return jnp.moveaxis(t.reshape((b, s // c, c, h) + t.shape[3:]), 3, 1)


def _from_chunks(t):
    b, h, n, c, d = t.shape
    return jnp.moveaxis(t, 1, 3).reshape(b, n * c, h, d)


def rg_lru(u, conv_w, conv_b, w_a, b_a, w_x, b_x, lam):
    b, s, _ = u.shape
    xc = (causal_depthwise_conv(u, conv_w) + conv_b).astype(jnp.float32)
    xblk = xc.reshape(b, s, LRU_BLOCKS, LRU_BLOCK_DIM)
    r = jax.nn.sigmoid(jnp.einsum("bsgi,gij->bsgj", xblk, w_a.astype(jnp.float32)).reshape(b, s, LRU_WIDTH) + b_a)
    i = jax.nn.sigmoid(jnp.einsum("bsgi,gij->bsgj", xblk, w_x.astype(jnp.float32)).reshape(b, s, LRU_WIDTH) + b_x)
    log_a = -LRU_C * r * jax.nn.softplus(-lam.astype(jnp.float32))
    a = jnp.exp(log_a)
    mult = jnp.sqrt(-jnp.expm1(2.0 * log_a))
    mult = jnp.where(jnp.arange(s)[None, :, None] == 0, 1.0, mult)
    inp = mult * i * xc

    def combine(lhs, rhs):
        return (lhs[0] * rhs[0], rhs[0] * lhs[1] + rhs[1])

    _, h = lax.associative_scan(combine, (a, inp), axis=1)
    return h


def gated_delta_rule_chunked(q, k, v, g, beta):
    b, s, h, dk = q.shape
    dv = v.shape[-1]
    c = GDN_CHUNK
    q, k, v = _to_chunks(q, c), _to_chunks(k, c), _to_chunks(v, c)
    g, beta = _to_chunks(g, c), _to_chunks(beta, c)
    gc = jnp.cumsum(g, axis=-1)
    causal = jnp.tril(jnp.ones((c, c), dtype=bool))
    strict = jnp.tril(jnp.ones((c, c), dtype=bool), k=-1)
    decay = jnp.exp(jnp.where(causal, gc[..., :, None] - gc[..., None, :], -jnp.inf))
    k_beta = k * beta[..., None]
    a_mat = jnp.where(strict, jnp.einsum("bhnik,bhnjk->bhnij", k_beta, k) * decay, 0.0)
    eye = jnp.broadcast_to(jnp.eye(c, dtype=a_mat.dtype), a_mat.shape)
    t_inv = lax.linalg.triangular_solve(a_mat, eye, left_side=True, lower=True, unit_diagonal=True)
    u = t_inv @ (v * beta[..., None])
    w = t_inv @ (k_beta * jnp.exp(gc)[..., None])
    q_dec = q * jnp.exp(gc)[..., None]
    attn = jnp.einsum("bhnik,bhnjk->bhnij", q, k) * decay
    k_dec = k * jnp.exp(gc[..., -1:] - gc)[..., None]
    g_last = jnp.exp(gc[..., -1])

    def step(state, xs):
        u_i, w_i, q_i, attn_i, kd_i, gl_i = xs
        v_new = u_i - w_i @ state
        o_i = q_i @ state + attn_i @ v_new
        state = state * gl_i[..., None, None] + jnp.swapaxes(kd_i, -1, -2) @ v_new
        return state, o_i

    xs = tuple(jnp.moveaxis(t, 2, 0) for t in (u, w, q_dec, attn, k_dec, g_last))
    state0 = jnp.zeros((b, h, dk, dv), q.dtype)
    _, o = lax.scan(step, state0, xs)
    return _from_chunks(jnp.moveaxis(o, 0, 2))


def gla_chunked(q, k, v, gk):
    b, s, h, dk = q.shape
    dv = v.shape[-1]
    c = GLA_CHUNK
    q, k, v, gk = (_to_chunks(t, c) for t in (q, k, v, gk))
    bcum = jnp.cumsum(gk, axis=-2)
    q_dec = q * jnp.exp(bcum)
    k_inv = k * jnp.exp(-bcum)
    causal = jnp.tril(jnp.ones((c, c), dtype=bool))
    attn = jnp.where(causal, jnp.einsum("bhnik,bhnjk->bhnij", q_dec, k_inv), 0.0)
    o_intra = attn @ v
    k_dec = k * jnp.exp(bcum[..., -1:, :] - bcum)
    g_last = jnp.exp(bcum[..., -1, :])

    def step(state, xs):
        q_i, kd_i, v_i, gl_i = xs
        o_i = q_i @ state
        state = state * gl_i[..., :, None] + jnp.swapaxes(kd_i, -1, -2) @ v_i
        return state, o_i

    xs = tuple(jnp.moveaxis(t, 2, 0) for t in (q_dec, k_dec, v, g_last))
    state0 = jnp.zeros((b, h, dk, dv), q.dtype)
    _, o_inter = lax.scan(step, state0, xs)
    return _from_chunks(o_intra + jnp.moveaxis(o_inter, 0, 2))


def stick_breaking_attention(q, k, v):
    b, s, h, dh = q.shape
    q = jnp.swapaxes(q, 1, 2) * dh ** -0.5
    k = jnp.swapaxes(k, 1, 2)
    v = jnp.swapaxes(v, 1, 2)
    outs = []
    for blk in range(s // SB_BLOCK):
        start = blk * SB_BLOCK
        end = start + SB_BLOCK
        z = jnp.einsum("bhqd,bhkd->bhqk", q[:, :, start:end], k[:, :, :end])
        strict = jnp.arange(end)[None, :] < (start + jnp.arange(SB_BLOCK))[:, None]
        log_keep = jnp.where(strict, jax.nn.log_sigmoid(-z), 0.0)
        between = lax.cumsum(log_keep, axis=3, reverse=True) - log_keep
        weights = jnp.where(strict, jnp.exp(jax.nn.log_sigmoid(z) + between), 0.0)
        outs.append(jnp.einsum("bhqk,bhkd->bhqd", weights, v[:, :, :end]))
    o = jnp.concatenate(outs, axis=2)
    return jnp.swapaxes(o, 1, 2).reshape(b, s, h * dh)


def mixer(x, w_in, conv_a_w, conv_a_b, rg_w_a, rg_b_a, rg_w_x, rg_b_x, rg_lambda,
          gdn_conv_w, gdn_a_log, gdn_dt_bias, gdn_norm_w,
          gla_w_gate_up, gla_b_gate, gla_norm_w, w_branch, w_out):
    b, s, _ = x.shape
    f32 = jnp.float32
    proj = jnp.einsum("bsd,de->bse", x, w_in)
    offsets = np.cumsum(IN_SPLITS)[:-1].tolist()
    (a_in, gdn_qkv, gdn_beta_in, gdn_decay_in, gdn_gate_in,
     sb_qkv, gla_qkv, gla_lr, gla_gate_in, merge_in) = jnp.split(proj, offsets, axis=-1)

    y_a = rg_lru(a_in, conv_a_w, conv_a_b, rg_w_a, rg_b_a, rg_w_x, rg_b_x, rg_lambda)

    qkv = jax.nn.silu(causal_depthwise_conv(gdn_qkv, gdn_conv_w)).astype(f32)
    q, k, v = jnp.split(qkv, [GDN_HEADS * GDN_DK, 2 * GDN_HEADS * GDN_DK], axis=-1)
    q = l2_normalize(q.reshape(b, s, GDN_HEADS, GDN_DK)) * GDN_DK ** -0.5
    k = l2_normalize(k.reshape(b, s, GDN_HEADS, GDN_DK))
    v = v.reshape(b, s, GDN_HEADS, GDN_DV)
    beta = jax.nn.sigmoid(gdn_beta_in.astype(f32))
    g = -jnp.exp(gdn_a_log.astype(f32)) * jax.nn.softplus(gdn_decay_in.astype(f32) + gdn_dt_bias)
    o_b = gated_delta_rule_chunked(q, k, v, g, beta)
    y_b = (rms_norm(o_b, gdn_norm_w) * jax.nn.silu(gdn_gate_in.reshape(b, s, GDN_HEADS, GDN_DV).astype(f32))).reshape(b, s, -1)

    q, k, v = jnp.split(sb_qkv.astype(f32), 3, axis=-1)
    y_c = stick_breaking_attention(q.reshape(b, s, SB_HEADS, SB_DH), k.reshape(b, s, SB_HEADS, SB_DH),
                                   v.reshape(b, s, SB_HEADS, SB_DH))

    q, k, v = jnp.split(gla_qkv.astype(f32), [GLA_HEADS * GLA_DK, 2 * GLA_HEADS * GLA_DK], axis=-1)
    gk = jax.nn.log_sigmoid(jnp.einsum("bsr,rc->bsc", gla_lr.astype(f32), gla_w_gate_up.astype(f32)) + gla_b_gate) / GLA_TAU
    o_d = gla_chunked(q.reshape(b, s, GLA_HEADS, GLA_DK) * GLA_DK ** -0.5, k.reshape(b, s, GLA_HEADS, GLA_DK),
                      v.reshape(b, s, GLA_HEADS, GLA_DV), gk.reshape(b, s, GLA_HEADS, GLA_DK))
    y_d = (rms_norm(o_d, gla_norm_w) * jax.nn.silu(gla_gate_in.reshape(b, s, GLA_HEADS, GLA_DV).astype(f32))).reshape(b, s, -1)

    gates = jax.nn.sigmoid(merge_in.reshape(b, s, N_BRANCHES, D_MODEL))
    merged = jnp.zeros_like(x)
    for n, y_n in enumerate((y_a, y_b, y_c, y_d)):
        merged = merged + gates[:, :, n] * jnp.einsum("bsc,cd->bsd", y_n.astype(x.dtype), w_branch[n])
    return jnp.einsum("bsd,de->bse", merged, w_out)


def moe(x, w_router, router_bias, w_gate, w_up, w_down):
    b, s, _ = x.shape
    scores = jax.nn.sigmoid(jnp.einsum("bsd,de->bse", x.astype(jnp.float32), w_router.astype(jnp.float32)))
    biased = scores + router_bias.astype(jnp.float32)
    grouped = biased.reshape(b, s, N_GROUPS, EXPERTS_PER_GROUP)
    group_score = jnp.sum(lax.top_k(grouped, TOP_K)[0], axis=-1)
    group_mask = jnp.argmax(group_score, axis=-1)[..., None] == jnp.arange(N_GROUPS)
    masked = jnp.where(group_mask[..., None], grouped, -jnp.inf).reshape(b, s, N_EXPERTS)
    _, expert_idx = lax.top_k(masked, TOP_K)
    w = jnp.take_along_axis(scores, expert_idx, axis=-1)
    w = w / jnp.sum(w, axis=-1, keepdims=True)
    combine = jnp.sum(jax.nn.one_hot(expert_idx, N_EXPERTS, dtype=jnp.float32) * w[..., None], axis=-2)
    combine = combine.astype(x.dtype)
    y = jnp.zeros_like(x)
    for e in range(N_EXPERTS):
        h = jax.nn.silu(jnp.einsum("bsd,df->bsf", x, w_gate[e])) * jnp.einsum("bsd,df->bsf", x, w_up[e])
        y = y + combine[..., e:e + 1] * jnp.einsum("bsf,fd->bsd", h, w_down[e])
    return y


def setup_inputs(seed: int = 0) -> dict:
    key = jax.random.key(seed)
    ks = jax.random.split(key, 32)
    f32 = jnp.float32
    L = DEPTH

    def nrm(i, shape, scale):
        return jax.random.normal(ks[i], shape, f32) * scale

    x = nrm(0, (BATCH, SEQ, D_MODEL), 1.0)
    w_in = nrm(1, (L, D_MODEL, D_IN), D_MODEL ** -0.5)
    conv_a_w = nrm(2, (L, CONV_WIDTH, LRU_WIDTH), CONV_WIDTH ** -0.5)
    conv_a_b = nrm(3, (L, LRU_WIDTH), 0.01)
    rg_w_a = nrm(4, (L, LRU_BLOCKS, LRU_BLOCK_DIM, LRU_BLOCK_DIM), LRU_BLOCK_DIM ** -0.5)
    rg_b_a = nrm(5, (L, LRU_WIDTH), 0.01)
    rg_w_x = nrm(6, (L, LRU_BLOCKS, LRU_BLOCK_DIM, LRU_BLOCK_DIM), LRU_BLOCK_DIM ** -0.5)
    rg_b_x = nrm(7, (L, LRU_WIDTH), 0.01)
    a_pow = jax.random.uniform(ks[8], (L, LRU_WIDTH), f32, 0.9, 0.999)
    a0 = a_pow ** (1.0 / LRU_C)
    rg_lambda = jnp.log(a0) - jnp.log1p(-a0)
    gdn_conv_w = nrm(9, (L, CONV_WIDTH, 3 * GDN_HEADS * GDN_DK), CONV_WIDTH ** -0.5)
    gdn_a_log = jnp.log(jax.random.uniform(ks[10], (L, GDN_HEADS), f32, 1.0, 16.0))
    dt = jnp.exp(jax.random.uniform(ks[11], (L, GDN_HEADS), f32, float(np.log(1e-3)), float(np.log(1e-1))))
    gdn_dt_bias = dt + jnp.log(-jnp.expm1(-dt))
    gdn_norm_w = 1.0 + nrm(12, (L, GDN_DV), 0.01)
    gla_w_gate_up = nrm(13, (L, GLA_GATE_RANK, GLA_HEADS * GLA_DK), GLA_GATE_RANK ** -0.5)
    gla_b_gate = nrm(14, (L, GLA_HEADS * GLA_DK), 0.01)
    gla_norm_w = 1.0 + nrm(15, (L, GLA_DV), 0.01)
    w_branch = nrm(16, (L, N_BRANCHES, BRANCH_WIDTH, D_MODEL), BRANCH_WIDTH ** -0.5)
    w_out = nrm(17, (L, D_MODEL, D_MODEL), D_MODEL ** -0.5 * DN_BETA)
    ln1_g = 1.0 + nrm(18, (L, D_MODEL), 0.01)
    ln1_b = nrm(19, (L, D_MODEL), 0.01)
    w_router = nrm(20, (D_MODEL, N_EXPERTS), D_MODEL ** -0.5)
    router_bias = nrm(21, (N_EXPERTS,), 0.01)
    w_gate = nrm(22, (L, N_EXPERTS, D_MODEL, D_EXPERT), D_MODEL ** -0.5)
    w_up = nrm(23, (L, N_EXPERTS, D_MODEL, D_EXPERT), D_MODEL ** -0.5)
    w_down = nrm(24, (L, N_EXPERTS, D_EXPERT, D_MODEL), D_EXPERT ** -0.5 * DN_BETA)
    ln2_g = 1.0 + nrm(25, (L, D_MODEL), 0.01)
    ln2_b = nrm(26, (L, D_MODEL), 0.01)
    return {"x": x, "w_in": w_in, "conv_a_w": conv_a_w, "conv_a_b": conv_a_b,
            "rg_w_a": rg_w_a, "rg_b_a": rg_b_a, "rg_w_x": rg_w_x, "rg_b_x": rg_b_x,
            "rg_lambda": rg_lambda, "gdn_conv_w": gdn_conv_w, "gdn_a_log": gdn_a_log,
            "gdn_dt_bias": gdn_dt_bias, "gdn_norm_w": gdn_norm_w, "gla_w_gate_up": gla_w_gate_up,
            "gla_b_gate": gla_b_gate, "gla_norm_w": gla_norm_w, "w_branch": w_branch,
            "w_out": w_out, "ln1_g": ln1_g, "ln1_b": ln1_b, "w_router": w_router,
            "router_bias": router_bias, "w_gate": w_gate, "w_up": w_up, "w_down": w_down,
            "ln2_g": ln2_g, "ln2_b": ln2_b}


def reference(x, w_in, conv_a_w, conv_a_b, rg_w_a, rg_b_a, rg_w_x, rg_b_x, rg_lambda,
              gdn_conv_w, gdn_a_log, gdn_dt_bias, gdn_norm_w, gla_w_gate_up, gla_b_gate,
              gla_norm_w, w_branch, w_out, ln1_g, ln1_b, w_router, router_bias,
              w_gate, w_up, w_down, ln2_g, ln2_b):
    for l in range(DEPTH):
        h = mixer(x, w_in[l], conv_a_w[l], conv_a_b[l], rg_w_a[l], rg_b_a[l], rg_w_x[l], rg_b_x[l],
                  rg_lambda[l], gdn_conv_w[l], gdn_a_log[l], gdn_dt_bias[l], gdn_norm_w[l],
                  gla_w_gate_up[l], gla_b_gate[l], gla_norm_w[l], w_branch[l], w_out[l])
        x = layer_norm(DN_ALPHA * x + h, ln1_g[l], ln1_b[l])
        h = moe(x, w_router, router_bias, w_gate[l], w_up[l], w_down[l])
        x = layer_norm(DN_ALPHA * x + h, ln2_g[l], ln2_b[l])
    return x
```

```python
import functools

import jax
import jax.numpy as jnp
from jax import lax
from jax.experimental import pallas as pl
from jax.experimental.pallas import tpu as pltpu

F32 = jnp.float32
BF16 = jnp.bfloat16

D_MODEL = 1024
DEPTH = 4
LRU_WIDTH = 512
LRU_BLOCKS = 8
LRU_C = 8.0
CONV_WIDTH = 4
GDN_HEADS = 4
GDN_DK = 128
GDN_DV = 128
GDN_CHUNK = 64
SB_HEADS = 8
SB_DH = 64
SB_BLOCK = 128
GLA_HEADS = 4
GLA_DK = 128
GLA_DV = 128
GLA_GATE_RANK = 16
GLA_TAU = 16.0
GLA_CHUNK = 64
N_BRANCHES = 4
BRANCH_WIDTH = 512
N_EXPERTS = 16
N_GROUPS = 4
EXPERTS_PER_GROUP = 4
D_EXPERT = 512
LN_EPS = 1e-5
NORM_EPS = 1e-6
DN_ALPHA = (2 * DEPTH) ** 0.25

_OFF_A = 0
_OFF_GDN_QKV = _OFF_A + LRU_WIDTH
_OFF_GDN_BETA = _OFF_GDN_QKV + 3 * GDN_HEADS * GDN_DK
_OFF_GDN_DECAY = _OFF_GDN_BETA + GDN_HEADS
_OFF_GDN_GATE = _OFF_GDN_DECAY + GDN_HEADS
_OFF_SB = _OFF_GDN_GATE + GDN_HEADS * GDN_DV
_OFF_GLA_QKV = _OFF_SB + 3 * SB_HEADS * SB_DH
_OFF_GLA_LR = _OFF_GLA_QKV + GLA_HEADS * (2 * GLA_DK + GLA_DV)
_OFF_GLA_GATE = _OFF_GLA_LR + GLA_GATE_RANK
_OFF_MERGE = _OFF_GLA_GATE + GLA_HEADS * GLA_DV

LANES = 128
SUBLANES = 8
ROW_TILE = 256
VMEM_LIMIT = 56 * 1024 * 1024


def _bdot(a, b):
    return jnp.dot(a.astype(BF16), b.astype(BF16), preferred_element_type=F32)


def _bdot_nt(a, b):
    return lax.dot_general(a.astype(BF16), b.astype(BF16), (((1,), (1,)), ((), ())),
                           preferred_element_type=F32)


def _bdot_tn(a, b):
    return lax.dot_general(a.astype(BF16), b.astype(BF16), (((0,), (0,)), ((), ())),
                           preferred_element_type=F32)


def _split_dot(a, b_bf16):
    hi = a.astype(BF16)
    lo = (a - hi.astype(F32)).astype(BF16)
    return (jnp.dot(hi, b_bf16, preferred_element_type=F32)
            + jnp.dot(lo, b_bf16, preferred_element_type=F32))


def _split_dot_nt(a_bf16, b):
    hi = b.astype(BF16)
    lo = (b - hi.astype(F32)).astype(BF16)
    dn = (((1,), (1,)), ((), ()))
    return (lax.dot_general(a_bf16, hi, dn, preferred_element_type=F32)
            + lax.dot_general(a_bf16, lo, dn, preferred_element_type=F32))


def _silu(x):
    return x * jax.nn.sigmoid(x)


def _layer_norm(x, g, b):
    mu = jnp.mean(x, axis=-1, keepdims=True)
    xc = x - mu
    var = jnp.mean(xc * xc, axis=-1, keepdims=True)
    return xc * lax.rsqrt(var + LN_EPS) * g + b


def _shift_rows(u, tail, s, row8):
    ur = pltpu.roll(u, s, 0)
    top = jnp.where(row8 >= s, ur[:SUBLANES], pltpu.roll(tail, s, 0))
    return jnp.concatenate([top, ur[SUBLANES:]], axis=0)


def _causal_conv(u, tail, cw_ref, row8):
    y = u * cw_ref[CONV_WIDTH - 1:CONV_WIDTH, :]
    for s in range(1, CONV_WIDTH):
        k = CONV_WIDTH - 1 - s
        y = y + _shift_rows(u, tail, s, row8) * cw_ref[k:k + 1, :]
    return y


def _lru_kernel(xb_ref, w_ref, cw_ref, cb_ref, wg_ref, bg_ref, lam_ref, o_ref, tail_ref, h_ref):
    t = pl.program_id(1)
    tt = xb_ref.shape[1]
    w = LRU_WIDTH

    @pl.when(t == 0)
    def _():
        tail_ref[...] = jnp.zeros_like(tail_ref)
        h_ref[...] = jnp.zeros_like(h_ref)

    u = jnp.dot(xb_ref[0], w_ref[...], preferred_element_type=F32)
    row = lax.broadcasted_iota(jnp.int32, (tt, w), 0)
    row8 = lax.broadcasted_iota(jnp.int32, (SUBLANES, w), 0)
    xc = _causal_conv(u, tail_ref[...], cw_ref, row8) + cb_ref[...]
    tail_ref[...] = u[tt - SUBLANES:, :]

    gates = _bdot(xc, wg_ref[...]) + bg_ref[...]
    r = jax.nn.sigmoid(gates[:, :w])
    i = jax.nn.sigmoid(gates[:, w:])
    log_a = (-LRU_C) * r * jax.nn.softplus(-lam_ref[...])
    a = jnp.exp(log_a)
    mult = jnp.sqrt(-jnp.tanh(log_a) * (a * a + 1.0))
    mult = jnp.where(row + t * tt == 0, 1.0, mult)
    b = mult * i * xc

    d = 1
    while d < tt:
        keep = row >= d
        a_sh = jnp.where(keep, pltpu.roll(a, d, 0), 1.0)
        b_sh = jnp.where(keep, pltpu.roll(b, d, 0), 0.0)
        b = a * b_sh + b
        a = a * a_sh
        d *= 2
    h = a * h_ref[0:1, :] + b
    h_ref[...] = jnp.broadcast_to(h[tt - 1:tt, :], h_ref.shape)
    o_ref[0] = h.astype(o_ref.dtype)


def _lru_call(xb, w, cw, cb, wg, bg, lam):
    bsz, s, d = xb.shape
    tt = ROW_TILE
    const = lambda shape: pl.BlockSpec(shape, lambda b, t: (0,) * len(shape))
    return pl.pallas_call(
        _lru_kernel,
        grid=(bsz, s // tt),
        in_specs=[pl.BlockSpec((1, tt, d), lambda b, t: (b, t, 0)),
                  const(w.shape), const(cw.shape), const(cb.shape), const(wg.shape),
                  const(bg.shape), const(lam.shape)],
        out_specs=pl.BlockSpec((1, tt, LRU_WIDTH), lambda b, t: (b, t, 0)),
        out_shape=jax.ShapeDtypeStruct((bsz, s, LRU_WIDTH), BF16),
        scratch_shapes=[pltpu.VMEM((SUBLANES, LRU_WIDTH), F32), pltpu.VMEM((SUBLANES, LRU_WIDTH), F32)],
        compiler_params=pltpu.CompilerParams(dimension_semantics=("arbitrary", "arbitrary"),
                                             vmem_limit_bytes=VMEM_LIMIT),
        name="mixer_lru",
    )(xb, w, cw, cb, wg, bg, lam)


def _sb_kernel(xb_ref, w_ref, u2_ref, o_ref, q_s, k_s, v_s):
    s = xb_ref.shape[1]
    blk = SB_BLOCK
    dh = SB_DH
    hw = 2 * dh

    def proj(t, c):
        r0 = pl.multiple_of(t * ROW_TILE, ROW_TILE)
        qkv = jnp.dot(xb_ref[0, pl.ds(r0, ROW_TILE), :], w_ref[0], preferred_element_type=F32)
        for hh in range(2):
            q_s[hh, pl.ds(r0, ROW_TILE), :] = (qkv[:, hh * dh:(hh + 1) * dh] * dh ** -0.5).astype(BF16)
            k_s[hh, pl.ds(r0, ROW_TILE), :] = qkv[:, hw + hh * dh:hw + (hh + 1) * dh].astype(BF16)
            v_s[hh, pl.ds(r0, ROW_TILE), :] = qkv[:, 2 * hw + hh * dh:2 * hw + (hh + 1) * dh].astype(BF16)
        return c

    lax.fori_loop(0, s // ROW_TILE, proj, 0)

    row = lax.broadcasted_iota(jnp.int32, (blk, blk), 0)
    col = lax.broadcasted_iota(jnp.int32, (blk, blk), 1)
    u2 = u2_ref[...]

    for hh in range(2):
        def qloop(qi, c, hh=hh):
            q0 = pl.multiple_of(qi * blk, blk)
            qb = q_s[hh, pl.ds(q0, blk), :]

            def kloop(t, carry):
                acc, run = carry
                k0 = pl.multiple_of((qi - t) * blk, blk)
                kb = k_s[hh, pl.ds(k0, blk), :]
                vb = v_s[hh, pl.ds(k0, blk), :]
                z = lax.dot_general(qb, kb, (((1,), (1,)), ((), ())), preferred_element_type=F32)
                sp = jax.nn.softplus(z)
                valid = col < row + jnp.minimum(t, 1) * blk
                lk = jnp.where(valid, -sp, 0.0)
                cs2 = _split_dot(lk, u2)
                between = cs2[:, :blk] - lk + run
                p = jnp.where(valid, jnp.exp(z - sp + between), 0.0)
                acc = acc + jnp.dot(p.astype(BF16), vb, preferred_element_type=F32)
                return acc, run + cs2[:, blk:]

            acc, _ = lax.fori_loop(0, qi + 1, kloop,
                                   (jnp.zeros((blk, dh), F32), jnp.zeros((blk, blk), F32)))
            o_ref[0, pl.ds(q0, blk), hh * dh:(hh + 1) * dh] = acc.astype(o_ref.dtype)
            return c

        lax.fori_loop(0, s // blk, qloop, 0)


def _sb_call(xb, w, u2):
    bsz, s, d = xb.shape
    npair = SB_HEADS // 2
    return pl.pallas_call(
        _sb_kernel,
        grid=(bsz, npair),
        in_specs=[pl.BlockSpec((1, s, d), lambda b, p: (b, 0, 0)),
                  pl.BlockSpec((1, d, 6 * SB_DH), lambda b, p: (p, 0, 0)),
                  pl.BlockSpec(u2.shape, lambda b, p: (0, 0))],
        out_specs=pl.BlockSpec((1, s, 2 * SB_DH), lambda b, p: (b, 0, p)),
        out_shape=jax.ShapeDtypeStruct((bsz, s, SB_HEADS * SB_DH), BF16),
        scratch_shapes=[pltpu.VMEM((2, s, SB_DH), BF16)] * 3,
        compiler_params=pltpu.CompilerParams(dimension_semantics=("arbitrary", "arbitrary"),
                                             vmem_limit_bytes=VMEM_LIMIT),
        name="mixer_sb",
    )(xb, w, u2)


def _gla_kernel(xb_ref, w_ref, wlr_ref, wgu_ref, bg_ref, nw_ref, tri_ref, o_ref,
                q_s, k_s, v_s, g_s, gate_s):
    s = xb_ref.shape[1]
    c = GLA_CHUNK
    dk = GLA_DK

    def proj(t, carry):
        r0 = pl.multiple_of(t * ROW_TILE, ROW_TILE)
        x = xb_ref[0, pl.ds(r0, ROW_TILE), :]
        p = jnp.dot(x, w_ref[0], preferred_element_type=F32)
        lr = jnp.dot(x, wlr_ref[...], preferred_element_type=F32)
        gpre = _bdot(lr, wgu_ref[0]) + bg_ref[0]
        g_s[pl.ds(r0, ROW_TILE), :] = jax.nn.log_sigmoid(gpre) * (1.0 / GLA_TAU)
        q_s[pl.ds(r0, ROW_TILE), :] = p[:, :dk] * dk ** -0.5
        k_s[pl.ds(r0, ROW_TILE), :] = p[:, dk:2 * dk]
        v_s[pl.ds(r0, ROW_TILE), :] = p[:, 2 * dk:3 * dk]
        gate_s[pl.ds(r0, ROW_TILE), :] = p[:, 3 * dk:]
        return carry

    lax.fori_loop(0, s // ROW_TILE, proj, 0)

    row = lax.broadcasted_iota(jnp.int32, (c, c), 0)
    col = lax.broadcasted_iota(jnp.int32, (c, c), 1)
    tri = tri_ref[...]

    def chunk(ci, st):
        r0 = pl.multiple_of(ci * c, c)
        g = g_s[pl.ds(r0, c), :]
        q = q_s[pl.ds(r0, c), :]
        k = k_s[pl.ds(r0, c), :]
        v = v_s[pl.ds(r0, c), :]
        bc = _split_dot_nt_left(tri, g)
        blast = bc[c - 1:c, :]
        qd = q * jnp.exp(bc)
        ki = k * jnp.exp(-bc)
        kd = k * jnp.exp(blast - bc)
        attn = jnp.where(col <= row, _bdot_nt(qd, ki), 0.0)
        o = _bdot(attn, v) + _bdot_nt(qd, st)
        st = st * jnp.exp(blast) + _bdot_tn(v, kd)
        ms = jnp.mean(o * o, axis=-1, keepdims=True)
        y = o * lax.rsqrt(ms + NORM_EPS) * nw_ref[...] * _silu(gate_s[pl.ds(r0, c), :])
        o_ref[0, pl.ds(r0, c), :] = y.astype(o_ref.dtype)
        return st

    lax.fori_loop(0, s // c, chunk, jnp.zeros((GLA_DV, dk), F32))


def _split_dot_nt_left(tri_bf16, x):
    hi = x.astype(BF16)
    lo = (x - hi.astype(F32)).astype(BF16)
    return (jnp.dot(tri_bf16, hi, preferred_element_type=F32)
            + jnp.dot(tri_bf16, lo, preferred_element_type=F32))


def _gla_call(xb, w, wlr, wgu, bg, nw, tri):
    bsz, s, d = xb.shape
    return pl.pallas_call(
        _gla_kernel,
        grid=(bsz, GLA_HEADS),
        in_specs=[pl.BlockSpec((1, s, d), lambda b, h: (b, 0, 0)),
                  pl.BlockSpec((1, d, 4 * GLA_DK), lambda b, h: (h, 0, 0)),
                  pl.BlockSpec(wlr.shape, lambda b, h: (0, 0)),
                  pl.BlockSpec((1, LANES, GLA_DK), lambda b, h: (h, 0, 0)),
                  pl.BlockSpec((1, 1, GLA_DK), lambda b, h: (h, 0, 0)),
                  pl.BlockSpec(nw.shape, lambda b, h: (0, 0)),
                  pl.BlockSpec(tri.shape, lambda b, h: (0, 0))],
        out_specs=pl.BlockSpec((1, s, GLA_DV), lambda b, h: (b, 0, h)),
        out_shape=jax.ShapeDtypeStruct((bsz, s, GLA_HEADS * GLA_DV), BF16),
        scratch_shapes=[pltpu.VMEM((s, GLA_DK), F32)] * 5,
        compiler_params=pltpu.CompilerParams(dimension_semantics=("arbitrary", "arbitrary"),
                                             vmem_limit_bytes=VMEM_LIMIT),
        name="mixer_gla",
    )(xb, w, wlr, wgu, bg, nw, tri)


def _gdn_kernel(xb_ref, w_ref, wbd_ref, cw_ref, hp_ref, nw_ref, tri_ref, o_ref,
                q_s, k_s, v_s, gate_s, beta_s, g_s):
    s = xb_ref.shape[1]
    c = GDN_CHUNK
    dk = GDN_DK
    row8 = lax.broadcasted_iota(jnp.int32, (SUBLANES, 3 * dk), 0)

    def proj(t, tail):
        r0 = pl.multiple_of(t * ROW_TILE, ROW_TILE)
        x = xb_ref[0, pl.ds(r0, ROW_TILE), :]
        p = jnp.dot(x, w_ref[0], preferred_element_type=F32)
        bd = jnp.dot(x, wbd_ref[0], preferred_element_type=F32)
        u = p[:, :3 * dk]
        qkv = _silu(_causal_conv(u, tail, cw_ref.at[0], row8))
        q = qkv[:, :dk]
        k = qkv[:, dk:2 * dk]
        q = q * lax.rsqrt(jnp.sum(q * q, axis=-1, keepdims=True) + NORM_EPS) * dk ** -0.5
        k = k * lax.rsqrt(jnp.sum(k * k, axis=-1, keepdims=True) + NORM_EPS)
        q_s[pl.ds(r0, ROW_TILE), :] = q
        k_s[pl.ds(r0, ROW_TILE), :] = k
        v_s[pl.ds(r0, ROW_TILE), :] = qkv[:, 2 * dk:]
        gate_s[pl.ds(r0, ROW_TILE), :] = p[:, 3 * dk:]
        beta = jax.nn.sigmoid(bd[:, 0:1])
        g = -jnp.exp(hp_ref[0, 0:1, :]) * jax.nn.softplus(bd[:, 1:2] + hp_ref[0, 1:2, :])
        beta_s[pl.ds(r0, ROW_TILE), :] = jnp.broadcast_to(beta, (ROW_TILE, dk))
        g_s[pl.ds(r0, ROW_TILE), :] = g
        return u[ROW_TILE - SUBLANES:, :]

    lax.fori_loop(0, s // ROW_TILE, proj, jnp.zeros((SUBLANES, 3 * dk), F32))

    row = lax.broadcasted_iota(jnp.int32, (c, c), 0)
    col = lax.broadcasted_iota(jnp.int32, (c, c), 1)
    eye = (row == col).astype(F32)
    tri = tri_ref[...]
    avg = jnp.full((c, dk), 1.0 / dk, BF16)

    def chunk(ci, st):
        r0 = pl.multiple_of(ci * c, c)
        q = q_s[pl.ds(r0, c), :]
        k = k_s[pl.ds(r0, c), :]
        v = v_s[pl.ds(r0, c), :]
        beta = beta_s[pl.ds(r0, c), :]
        gc = _split_dot_nt_left(tri, g_s[pl.ds(r0, c), :])
        gc_row = _split_dot_nt(avg, gc)
        dec = jnp.exp(jnp.minimum(gc[:, :c] - gc_row, 0.0))
        kb = k * beta
        qk = _bdot_nt(jnp.concatenate([kb, q], axis=0), k)
        a = jnp.where(col < row, qk[:c] * dec, 0.0)
        attn = jnp.where(col <= row, qk[c:] * dec, 0.0)
        tinv = eye - a
        ak = a
        n = 1
        while n < c // 2:
            ak = _bdot(ak, ak)
            tinv = tinv + _bdot(tinv, ak)
            n *= 2
        egc = jnp.exp(gc)
        uw = _bdot(tinv, jnp.concatenate([v * beta, kb * egc], axis=1))
        u = uw[:, :GDN_DV]
        w = uw[:, GDN_DV:]
        gl = gc[c - 1:c, :]
        ws_qs = _bdot(jnp.concatenate([w, q * egc], axis=0), st)
        v_new = u - ws_qs[:c]
        o = ws_qs[c:] + _bdot(attn, v_new)
        st = st * jnp.exp(gl) + _bdot_tn(k * jnp.exp(gl - gc), v_new)
        ms = jnp.mean(o * o, axis=-1, keepdims=True)
        y = o * lax.rsqrt(ms + NORM_EPS) * nw_ref[...] * _silu(gate_s[pl.ds(r0, c), :])
        o_ref[0, pl.ds(r0, c), :] = y.astype(o_ref.dtype)
        return st

    lax.fori_loop(0, s // c, chunk, jnp.zeros((dk, GDN_DV), F32))


def _gdn_call(xb, w, wbd, cw, hp, nw, tri):
    bsz, s, d = xb.shape
    return pl.pallas_call(
        _gdn_kernel,
        grid=(bsz, GDN_HEADS),
        in_specs=[pl.BlockSpec((1, s, d), lambda b, h: (b, 0, 0)),
                  pl.BlockSpec((1, d, 4 * GDN_DK), lambda b, h: (h, 0, 0)),
                  pl.BlockSpec((1, d, LANES), lambda b, h: (h, 0, 0)),
                  pl.BlockSpec((1, CONV_WIDTH, 3 * GDN_DK), lambda b, h: (h, 0, 0)),
                  pl.BlockSpec((1, SUBLANES, GDN_DK), lambda b, h: (h, 0, 0)),
                  pl.BlockSpec(nw.shape, lambda b, h: (0, 0)),
                  pl.BlockSpec(tri.shape, lambda b, h: (0, 0))],
        out_specs=pl.BlockSpec((1, s, GDN_DV), lambda b, h: (b, 0, h)),
        out_shape=jax.ShapeDtypeStruct((bsz, s, GDN_HEADS * GDN_DV), BF16),
        scratch_shapes=[pltpu.VMEM((s, GDN_DK), F32)] * 6,
        compiler_params=pltpu.CompilerParams(dimension_semantics=("arbitrary", "arbitrary"),
                                             vmem_limit_bytes=VMEM_LIMIT),
        name="mixer_gdn",
    )(xb, w, wbd, cw, hp, nw, tri)


def _merge_kernel(x_ref, xb_ref, ya_ref, yb_ref, yc_ref, yd_ref, wm_ref, wb_ref, wo_ref,
                  g_ref, b_ref, o_ref, ob_ref):
    xb = xb_ref[...]
    merged = None
    for n, y_ref in enumerate((ya_ref, yb_ref, yc_ref, yd_ref)):
        gate = jax.nn.sigmoid(jnp.dot(xb, wm_ref[:, n * D_MODEL:(n + 1) * D_MODEL],
                                      preferred_element_type=F32))
        term = gate * jnp.dot(y_ref[...], wb_ref[n], preferred_element_type=F32)
        merged = term if merged is None else merged + term
    h = _bdot(merged, wo_ref[...])
    y = _layer_norm(DN_ALPHA * x_ref[...] + h, g_ref[...], b_ref[...])
    o_ref[...] = y
    ob_ref[...] = y.astype(BF16)


def _merge_call(x, xb, ya, yb, yc, yd, wm, wb, wo, g, b):
    t, d = x.shape
    tm = 256
    tok = lambda width: pl.BlockSpec((tm, width), lambda i: (i, 0))
    const = lambda shape: pl.BlockSpec(shape, lambda i: (0,) * len(shape),
                                       pipeline_mode=pl.Buffered(1))
    return pl.pallas_call(
        _merge_kernel,
        grid=(t // tm,),
        in_specs=[tok(d), tok(d), tok(BRANCH_WIDTH), tok(BRANCH_WIDTH), tok(BRANCH_WIDTH),
                  tok(BRANCH_WIDTH), const(wm.shape), const(wb.shape), const(wo.shape),
                  const(g.shape), const(b.shape)],
        out_specs=[tok(d), tok(d)],
        out_shape=[jax.ShapeDtypeStruct((t, d), F32), jax.ShapeDtypeStruct((t, d), BF16)],
        compiler_params=pltpu.CompilerParams(dimension_semantics=("arbitrary",),
                                             vmem_limit_bytes=VMEM_LIMIT),
        name="merge",
    )(x, xb, ya, yb, yc, yd, wm, wb, wo, g, b)


def _route(scores, biased):
    s = [scores[:, e:e + 1] for e in range(N_EXPERTS)]
    b = [biased[:, e:e + 1] for e in range(N_EXPERTS)]
    gs = []
    for g in range(N_GROUPS):
        m = b[EXPERTS_PER_GROUP * g:EXPERTS_PER_GROUP * (g + 1)]
        best = None
        for i in range(EXPERTS_PER_GROUP):
            for j in range(i + 1, EXPERTS_PER_GROUP):
                ps = m[i] + m[j]
                best = ps if best is None else jnp.maximum(best, ps)
        gs.append(best)
    gidx = jnp.zeros_like(gs[0], dtype=jnp.int32)
    gval = gs[0]
    for g in range(1, N_GROUPS):
        take = gs[g] > gval
        gidx = jnp.where(take, g, gidx)
        gval = jnp.where(take, gs[g], gval)
    w = []
    for e in range(N_EXPERTS):
        g = e // EXPERTS_PER_GROUP
        beaten = jnp.zeros_like(gidx)
        for j in range(EXPERTS_PER_GROUP * g, EXPERTS_PER_GROUP * (g + 1)):
            if j == e:
                continue
            wins = (b[j] >= b[e]) if j < e else (b[j] > b[e])
            beaten = beaten + wins.astype(jnp.int32)
        sel = (gidx == g) & (beaten < 2)
        w.append(jnp.where(sel, s[e], 0.0))
    tot = w[0]
    for e in range(1, N_EXPERTS):
        tot = tot + w[e]
    lane = lax.broadcasted_iota(jnp.int32, scores.shape, 1)
    comb = jnp.zeros_like(scores)
    for e in range(N_EXPERTS):
        comb = jnp.where(lane == e, w[e] / tot, comb)
    return comb


def _moe_kernel(x_ref, xb_ref, wr_ref, rb_ref, wg_ref, wu_ref, wd_ref, g_ref, b_ref,
                o_ref, ob_ref, acc_ref, comb_ref):
    e = pl.program_id(1)

    @pl.when(e == 0)
    def _():
        logits = jnp.dot(x_ref[...], wr_ref[...], precision=lax.Precision.HIGHEST,
                         preferred_element_type=F32)
        scores = jax.nn.sigmoid(logits)
        comb_ref[...] = _route(scores, scores + rb_ref[...])
        acc_ref[...] = jnp.zeros_like(acc_ref)

    xb = xb_ref[...]
    hg = jnp.dot(xb, wg_ref[0], preferred_element_type=F32)
    hu = jnp.dot(xb, wu_ref[0], preferred_element_type=F32)
    y = _bdot(_silu(hg) * hu, wd_ref[0])
    lane = lax.broadcasted_iota(jnp.int32, comb_ref.shape, 1)
    ce = jnp.sum(jnp.where(lane == e, comb_ref[...], 0.0), axis=-1, keepdims=True)
    acc_ref[...] += ce * y

    @pl.when(e == N_EXPERTS - 1)
    def _():
        out = _layer_norm(DN_ALPHA * x_ref[...] + acc_ref[...], g_ref[...], b_ref[...])
        o_ref[...] = out
        ob_ref[...] = out.astype(BF16)


def _moe_call(x, xb, wr, rb, wg, wu, wd, g, b):
    t, d = x.shape
    tm = 512
    tok = lambda: pl.BlockSpec((tm, d), lambda i, e: (i, 0))
    const = lambda shape: pl.BlockSpec(shape, lambda i, e: (0,) * len(shape))
    return pl.pallas_call(
        _moe_kernel,
        grid=(t // tm, N_EXPERTS),
        in_specs=[tok(), tok(), const(wr.shape), const(rb.shape),
                  pl.BlockSpec((1, d, D_EXPERT), lambda i, e: (e, 0, 0)),
                  pl.BlockSpec((1, d, D_EXPERT), lambda i, e: (e, 0, 0)),
                  pl.BlockSpec((1, D_EXPERT, d), lambda i, e: (e, 0, 0)),
                  const(g.shape), const(b.shape)],
        out_specs=[tok(), tok()],
        out_shape=[jax.ShapeDtypeStruct((t, d), F32), jax.ShapeDtypeStruct((t, d), BF16)],
        scratch_shapes=[pltpu.VMEM((tm, d), F32), pltpu.VMEM((tm, LANES), F32)],
        compiler_params=pltpu.CompilerParams(dimension_semantics=("arbitrary", "arbitrary"),
                                             vmem_limit_bytes=VMEM_LIMIT),
        name="moe",
    )(x, xb, wr, rb, wg, wu, wd, g, b)


def _block_diag(w):
    g, n, _ = w.shape
    eye = jnp.eye(g, dtype=w.dtype)
    return (eye[:, None, :, None] * w[:, :, None, :]).reshape(g * n, g * n)


def _per_head(w, off, n_heads, width, parts):
    cols = [w[:, off + p * n_heads * width: off + (p + 1) * n_heads * width]
            .reshape(w.shape[0], n_heads, width) for p in parts]
    return jnp.transpose(jnp.concatenate(cols, axis=2), (1, 0, 2))


def _row(v):
    return v.reshape(1, -1).astype(F32)


def kernel(x, w_in, conv_a_w, conv_a_b, rg_w_a, rg_b_a, rg_w_x, rg_b_x, rg_lambda, gdn_conv_w, gdn_a_log, gdn_dt_bias, gdn_norm_w, gla_w_gate_up, gla_b_gate, gla_norm_w, w_branch, w_out, ln1_g, ln1_b, w_router, router_bias, w_gate, w_up, w_down, ln2_g, ln2_b):
    bsz, s, d = x.shape
    t = bsz * s
    c = GDN_CHUNK
    ii = jnp.arange(SB_BLOCK)
    u2 = jnp.concatenate([(ii[:, None] >= ii[None, :]).astype(BF16),
                          jnp.ones((SB_BLOCK, SB_BLOCK), BF16)], axis=1)
    jj = jnp.arange(c)
    tri = (jj[:, None] >= jj[None, :]).astype(BF16)
    wr = jnp.pad(w_router.astype(F32), ((0, 0), (0, LANES - N_EXPERTS)))
    rb = jnp.pad(router_bias.astype(F32), (0, LANES - N_EXPERTS)).reshape(1, LANES)

    xf = x.reshape(t, d)
    xb = xf.astype(BF16)
    for l in range(DEPTH):
        wl = w_in[l]
        wlb = wl.astype(BF16)
        xb3 = xb.reshape(bsz, s, d)

        wg = jnp.concatenate([_block_diag(rg_w_a[l]), _block_diag(rg_w_x[l])], axis=1).astype(BF16)
        bg = jnp.concatenate([rg_b_a[l], rg_b_x[l]]).reshape(1, -1).astype(F32)
        ya = _lru_call(xb3, wlb[:, _OFF_A:_OFF_A + LRU_WIDTH], conv_a_w[l].astype(F32),
                       _row(conv_a_b[l]), wg, bg, _row(rg_lambda[l]))

        w_gdn = jnp.concatenate([_per_head(wlb, _OFF_GDN_QKV, GDN_HEADS, GDN_DK, (0, 1, 2)),
                                 _per_head(wlb, _OFF_GDN_GATE, GDN_HEADS, GDN_DV, (0,))], axis=2)
        wbd = jnp.stack([wlb[:, _OFF_GDN_BETA:_OFF_GDN_BETA + GDN_HEADS],
                         wlb[:, _OFF_GDN_DECAY:_OFF_GDN_DECAY + GDN_HEADS]], axis=2)
        wbd = jnp.pad(jnp.transpose(wbd, (1, 0, 2)), ((0, 0), (0, 0), (0, LANES - 2)))
        cw_gdn = jnp.transpose(gdn_conv_w[l].astype(F32).reshape(CONV_WIDTH, 3, GDN_HEADS, GDN_DK),
                               (2, 0, 1, 3)).reshape(GDN_HEADS, CONV_WIDTH, 3 * GDN_DK)
        hp = jnp.zeros((GDN_HEADS, SUBLANES, GDN_DK), F32)
        hp = hp.at[:, 0, :].set(gdn_a_log[l].astype(F32)[:, None])
        hp = hp.at[:, 1, :].set(gdn_dt_bias[l].astype(F32)[:, None])
        yb = _gdn_call(xb3, w_gdn, wbd, cw_gdn, hp, _row(gdn_norm_w[l]), tri)

        w_sb = _per_head(wlb, _OFF_SB, SB_HEADS // 2, 2 * SB_DH, (0, 1, 2))
        yc = _sb_call(xb3, w_sb, u2)

        q_off = _OFF_GLA_QKV
        w_gla = jnp.concatenate([
            _per_head(wlb, q_off, GLA_HEADS, GLA_DK, (0, 1)),
            _per_head(wlb, q_off + 2 * GLA_HEADS * GLA_DK, GLA_HEADS, GLA_DV, (0,)),
            _per_head(wlb, _OFF_GLA_GATE, GLA_HEADS, GLA_DV, (0,))], axis=2)
        wlr = jnp.pad(wlb[:, _OFF_GLA_LR:_OFF_GLA_LR + GLA_GATE_RANK],
                      ((0, 0), (0, LANES - GLA_GATE_RANK)))
        wgu = jnp.pad(jnp.transpose(gla_w_gate_up[l].reshape(GLA_GATE_RANK, GLA_HEADS, GLA_DK), (1, 0, 2)),
                      ((0, 0), (0, LANES - GLA_GATE_RANK), (0, 0))).astype(BF16)
        bgl = gla_b_gate[l].astype(F32).reshape(GLA_HEADS, 1, GLA_DK)
        yd = _gla_call(xb3, w_gla, wlr, wgu, bgl, _row(gla_norm_w[l]), tri)

        xf, xb = _merge_call(xf, xb, ya.reshape(t, -1), yb.reshape(t, -1), yc.reshape(t, -1),
                             yd.reshape(t, -1), wlb[:, _OFF_MERGE:], w_branch[l].astype(BF16),
                             w_out[l].astype(BF16), _row(ln1_g[l]), _row(ln1_b[l]))

        xf, xb = _moe_call(xf, xb, wr, rb, w_gate[l].astype(BF16), w_up[l].astype(BF16),
                           w_down[l].astype(BF16), _row(ln2_g[l]), _row(ln2_b[l]))
    return xf.reshape(bsz, s, d)
```

```python
import functools

import jax
import jax.numpy as jnp
from jax import lax
from jax.experimental import pallas as pl
from jax.experimental.pallas import tpu as pltpu

F32 = jnp.float32
BF16 = jnp.bfloat16

D_MODEL = 1024
DEPTH = 4
LRU_WIDTH = 512
LRU_BLOCKS = 8
LRU_C = 8.0
CONV_WIDTH = 4
GDN_HEADS = 4
GDN_DK = 128
GDN_DV = 128
GDN_CHUNK = 64
SB_HEADS = 8
SB_DH = 64
SB_BLOCK = 128
GLA_HEADS = 4
GLA_DK = 128
GLA_DV = 128
GLA_GATE_RANK = 16
GLA_TAU = 16.0
GLA_CHUNK = 64
N_BRANCHES = 4
BRANCH_WIDTH = 512
N_EXPERTS = 16
N_GROUPS = 4
EXPERTS_PER_GROUP = 4
D_EXPERT = 512
LN_EPS = 1e-5
NORM_EPS = 1e-6
DN_ALPHA = (2 * DEPTH) ** 0.25

_OFF_A = 0
_OFF_GDN_QKV = _OFF_A + LRU_WIDTH
_OFF_GDN_BETA = _OFF_GDN_QKV + 3 * GDN_HEADS * GDN_DK
_OFF_GDN_DECAY = _OFF_GDN_BETA + GDN_HEADS
_OFF_GDN_GATE = _OFF_GDN_DECAY + GDN_HEADS
_OFF_SB = _OFF_GDN_GATE + GDN_HEADS * GDN_DV
_OFF_GLA_QKV = _OFF_SB + 3 * SB_HEADS * SB_DH
_OFF_GLA_LR = _OFF_GLA_QKV + GLA_HEADS * (2 * GLA_DK + GLA_DV)
_OFF_GLA_GATE = _OFF_GLA_LR + GLA_GATE_RANK
_OFF_MERGE = _OFF_GLA_GATE + GLA_HEADS * GLA_DV

LANES = 128
SUBLANES = 8
ROW_TILE = 256
CHUNK_UNROLL = 8
VMEM_LIMIT = 56 * 1024 * 1024


def _bdot(a, b):
    return jnp.dot(a.astype(BF16), b.astype(BF16), preferred_element_type=F32)


def _bdot_nt(a, b):
    return lax.dot_general(a.astype(BF16), b.astype(BF16), (((1,), (1,)), ((), ())),
                           preferred_element_type=F32)


def _bdot_tn(a, b):
    return lax.dot_general(a.astype(BF16), b.astype(BF16), (((0,), (0,)), ((), ())),
                           preferred_element_type=F32)


def _split_dot(a, b_bf16):
    hi = a.astype(BF16)
    lo = (a - hi.astype(F32)).astype(BF16)
    return (jnp.dot(hi, b_bf16, preferred_element_type=F32)
            + jnp.dot(lo, b_bf16, preferred_element_type=F32))


def _split_dot_nt(a_bf16, b):
    hi = b.astype(BF16)
    lo = (b - hi.astype(F32)).astype(BF16)
    dn = (((1,), (1,)), ((), ()))
    return (lax.dot_general(a_bf16, hi, dn, preferred_element_type=F32)
            + lax.dot_general(a_bf16, lo, dn, preferred_element_type=F32))


def _silu(x):
    return x * jax.nn.sigmoid(x)


def _layer_norm(x, g, b):
    mu = jnp.mean(x, axis=-1, keepdims=True)
    xc = x - mu
    var = jnp.mean(xc * xc, axis=-1, keepdims=True)
    return xc * lax.rsqrt(var + LN_EPS) * g + b


def _shift_rows(u, tail, s, row8):
    ur = pltpu.roll(u, s, 0)
    top = jnp.where(row8 >= s, ur[:SUBLANES], pltpu.roll(tail, s, 0))
    return jnp.concatenate([top, ur[SUBLANES:]], axis=0)


def _causal_conv(u, tail, cw_ref, row8):
    y = u * cw_ref[CONV_WIDTH - 1:CONV_WIDTH, :]
    for s in range(1, CONV_WIDTH):
        k = CONV_WIDTH - 1 - s
        y = y + _shift_rows(u, tail, s, row8) * cw_ref[k:k + 1, :]
    return y


def _lru_kernel(xb_ref, w_ref, cw_ref, cb_ref, wg_ref, bg_ref, lam_ref, o_ref, tail_ref, h_ref):
    t = pl.program_id(1)
    tt = xb_ref.shape[1]
    w = LRU_WIDTH

    @pl.when(t == 0)
    def _():
        tail_ref[...] = jnp.zeros_like(tail_ref)
        h_ref[...] = jnp.zeros_like(h_ref)

    u = jnp.dot(xb_ref[0], w_ref[...], preferred_element_type=F32)
    row = lax.broadcasted_iota(jnp.int32, (tt, w), 0)
    row8 = lax.broadcasted_iota(jnp.int32, (SUBLANES, w), 0)
    xc = _causal_conv(u, tail_ref[...], cw_ref, row8) + cb_ref[...]
    tail_ref[...] = u[tt - SUBLANES:, :]

    gates = _bdot(xc, wg_ref[...]) + bg_ref[...]
    r = jax.nn.sigmoid(gates[:, :w])
    i = jax.nn.sigmoid(gates[:, w:])
    log_a = (-LRU_C) * r * jax.nn.softplus(-lam_ref[...])
    a = jnp.exp(log_a)
    mult = jnp.sqrt(-jnp.tanh(log_a) * (a * a + 1.0))
    mult = jnp.where(row + t * tt == 0, 1.0, mult)
    b = mult * i * xc

    d = 1
    while d < tt:
        keep = row >= d
        a_sh = jnp.where(keep, pltpu.roll(a, d, 0), 1.0)
        b_sh = jnp.where(keep, pltpu.roll(b, d, 0), 0.0)
        b = a * b_sh + b
        a = a * a_sh
        d *= 2
    h = a * h_ref[0:1, :] + b
    h_ref[...] = jnp.broadcast_to(h[tt - 1:tt, :], h_ref.shape)
    o_ref[0] = h.astype(o_ref.dtype)


def _lru_call(xb, w, cw, cb, wg, bg, lam):
    bsz, s, d = xb.shape
    tt = ROW_TILE
    const = lambda shape: pl.BlockSpec(shape, lambda b, t: (0,) * len(shape))
    return pl.pallas_call(
        _lru_kernel,
        grid=(bsz, s // tt),
        in_specs=[pl.BlockSpec((1, tt, d), lambda b, t: (b, t, 0)),
                  const(w.shape), const(cw.shape), const(cb.shape), const(wg.shape),
                  const(bg.shape), const(lam.shape)],
        out_specs=pl.BlockSpec((1, tt, LRU_WIDTH), lambda b, t: (b, t, 0)),
        out_shape=jax.ShapeDtypeStruct((bsz, s, LRU_WIDTH), BF16),
        scratch_shapes=[pltpu.VMEM((SUBLANES, LRU_WIDTH), F32), pltpu.VMEM((SUBLANES, LRU_WIDTH), F32)],
        compiler_params=pltpu.CompilerParams(dimension_semantics=("arbitrary", "arbitrary"),
                                             vmem_limit_bytes=VMEM_LIMIT),
        name="mixer_lru",
    )(xb, w, cw, cb, wg, bg, lam)


def _sb_kernel(xb_ref, w_ref, u2_ref, o_ref, q_s, k_s, v_s, acc_s, run_s):
    s = xb_ref.shape[1]
    blk = SB_BLOCK
    dh = SB_DH
    nh = SB_HEADS
    hd = nh * dh

    def proj(t, c):
        r0 = pl.multiple_of(t * ROW_TILE, ROW_TILE)
        qkv = jnp.dot(xb_ref[0, pl.ds(r0, ROW_TILE), :], w_ref[...], preferred_element_type=F32)
        for h in range(nh):
            q_s[h, pl.ds(r0, ROW_TILE), :] = (qkv[:, h * dh:(h + 1) * dh] * dh ** -0.5).astype(BF16)
            k_s[h, pl.ds(r0, ROW_TILE), :] = qkv[:, hd + h * dh:hd + (h + 1) * dh].astype(BF16)
            v_s[h, pl.ds(r0, ROW_TILE), :] = qkv[:, 2 * hd + h * dh:2 * hd + (h + 1) * dh].astype(BF16)
        return c

    lax.fori_loop(0, s // ROW_TILE, proj, 0)

    row = lax.broadcasted_iota(jnp.int32, (blk, blk), 0)
    col = lax.broadcasted_iota(jnp.int32, (blk, blk), 1)
    u2 = u2_ref[...]

    def qloop(qi, c):
        q0 = pl.multiple_of(qi * blk, blk)
        acc_s[...] = jnp.zeros_like(acc_s)
        run_s[...] = jnp.zeros_like(run_s)

        def kloop(t, c2):
            k0 = pl.multiple_of((qi - t) * blk, blk)
            valid = col < row + jnp.minimum(t, 1) * blk
            hs = range(nh)
            nt = (((1,), (1,)), ((), ()))
            z = [lax.dot_general(q_s[h, pl.ds(q0, blk), :], k_s[h, pl.ds(k0, blk), :], nt,
                                 preferred_element_type=F32) for h in hs]
            lk = [jnp.where(valid, -(jnp.maximum(z[h], 0.0) + jnp.log(1.0 + jnp.exp(-jnp.abs(z[h])))), 0.0)
                  for h in hs]
            cs2 = [_split_dot(lk[h], u2) for h in hs]
            p = [jnp.where(valid, jnp.exp(z[h] + cs2[h][:, :blk] + run_s[h]), 0.0) for h in hs]
            pv = [jnp.dot(p[h].astype(BF16), v_s[h, pl.ds(k0, blk), :], preferred_element_type=F32)
                  for h in hs]
            for h in hs:
                acc_s[h] += pv[h]
                run_s[h] += cs2[h][:, blk:]
            return c2

        lax.fori_loop(0, qi + 1, kloop, 0)
        for h in range(nh):
            o_ref[0, pl.ds(q0, blk), h * dh:(h + 1) * dh] = acc_s[h].astype(o_ref.dtype)
        return c

    lax.fori_loop(0, s // blk, qloop, 0)


def _sb_call(xb, w, u2):
    bsz, s, d = xb.shape
    return pl.pallas_call(
        _sb_kernel,
        grid=(bsz,),
        in_specs=[pl.BlockSpec((1, s, d), lambda b: (b, 0, 0)),
                  pl.BlockSpec(w.shape, lambda b: (0, 0)),
                  pl.BlockSpec(u2.shape, lambda b: (0, 0))],
        out_specs=pl.BlockSpec((1, s, SB_HEADS * SB_DH), lambda b: (b, 0, 0)),
        out_shape=jax.ShapeDtypeStruct((bsz, s, SB_HEADS * SB_DH), BF16),
        scratch_shapes=[pltpu.VMEM((SB_HEADS, s, SB_DH), BF16)] * 3
        + [pltpu.VMEM((SB_HEADS, SB_BLOCK, SB_DH), F32), pltpu.VMEM((SB_HEADS, SB_BLOCK, SB_BLOCK), F32)],
        compiler_params=pltpu.CompilerParams(dimension_semantics=("arbitrary",),
                                             vmem_limit_bytes=VMEM_LIMIT),
        name="mixer_sb",
    )(xb, w, u2)


def _gla_kernel(xb_ref, w_ref, wlr_ref, wgu_ref, bg_ref, nw_ref, tri_ref, o_ref,
                q_s, k_s, v_s, g_s, gate_s, oi_s, qd_s, m_s, eb_s):
    s = xb_ref.shape[1]
    c = GLA_CHUNK
    dk = GLA_DK
    nck = s // c

    def proj(t, carry):
        r0 = pl.multiple_of(t * ROW_TILE, ROW_TILE)
        x = xb_ref[0, pl.ds(r0, ROW_TILE), :]
        p = jnp.dot(x, w_ref[0], preferred_element_type=F32)
        lr = jnp.dot(x, wlr_ref[...], preferred_element_type=F32)
        gpre = _bdot(lr, wgu_ref[0]) + bg_ref[0]
        g_s[pl.ds(r0, ROW_TILE), :] = jax.nn.log_sigmoid(gpre) * (1.0 / GLA_TAU)
        q_s[pl.ds(r0, ROW_TILE), :] = p[:, :dk] * dk ** -0.5
        k_s[pl.ds(r0, ROW_TILE), :] = p[:, dk:2 * dk]
        v_s[pl.ds(r0, ROW_TILE), :] = p[:, 2 * dk:3 * dk]
        gate_s[pl.ds(r0, ROW_TILE), :] = p[:, 3 * dk:]
        return carry

    lax.fori_loop(0, s // ROW_TILE, proj, 0)

    row = lax.broadcasted_iota(jnp.int32, (c, c), 0)
    col = lax.broadcasted_iota(jnp.int32, (c, c), 1)
    tri = tri_ref[...]

    def intra(grp, carry):
        u = range(CHUNK_UNROLL)
        ci = [grp * CHUNK_UNROLL + j for j in u]
        r0 = [pl.multiple_of(ci[j] * c, c) for j in u]
        q = [q_s[pl.ds(r0[j], c), :] for j in u]
        k = [k_s[pl.ds(r0[j], c), :] for j in u]
        v = [v_s[pl.ds(r0[j], c), :] for j in u]
        bc = [_split_dot_nt_left(tri, g_s[pl.ds(r0[j], c), :]) for j in u]
        blast = [bc[j][c - 1:c, :] for j in u]
        qd = [q[j] * jnp.exp(bc[j]) for j in u]
        ki = [k[j] * jnp.exp(-bc[j]) for j in u]
        kd = [k[j] * jnp.exp(blast[j] - bc[j]) for j in u]
        attn = [jnp.where(col <= row, _bdot_nt(qd[j], ki[j]), 0.0) for j in u]
        oi = [_bdot(attn[j], v[j]) for j in u]
        m = [_bdot_tn(v[j], kd[j]) for j in u]
        for j in u:
            oi_s[pl.ds(r0[j], c), :] = oi[j]
            qd_s[pl.ds(r0[j], c), :] = qd[j].astype(BF16)
            m_s[ci[j]] = m[j]
            eb_s[pl.ds(ci[j], 1), :] = jnp.exp(blast[j])
        return carry

    lax.fori_loop(0, nck // CHUNK_UNROLL, intra, 0)

    def inter(grp, st):
        for j in range(CHUNK_UNROLL):
            ci = grp * CHUNK_UNROLL + j
            r0 = pl.multiple_of(ci * c, c)
            o = oi_s[pl.ds(r0, c), :] + lax.dot_general(
                qd_s[pl.ds(r0, c), :], st.astype(BF16), (((1,), (1,)), ((), ())),
                preferred_element_type=F32)
            st = st * eb_s[pl.ds(ci, 1), :] + m_s[ci]
            ms = jnp.mean(o * o, axis=-1, keepdims=True)
            y = o * lax.rsqrt(ms + NORM_EPS) * nw_ref[...] * _silu(gate_s[pl.ds(r0, c), :])
            o_ref[0, pl.ds(r0, c), :] = y.astype(o_ref.dtype)
        return st

    lax.fori_loop(0, nck // CHUNK_UNROLL, inter, jnp.zeros((GLA_DV, dk), F32))


def _split_dot_nt_left(tri_bf16, x):
    hi = x.astype(BF16)
    lo = (x - hi.astype(F32)).astype(BF16)
    return (jnp.dot(tri_bf16, hi, preferred_element_type=F32)
            + jnp.dot(tri_bf16, lo, preferred_element_type=F32))


def _gla_call(xb, w, wlr, wgu, bg, nw, tri):
    bsz, s, d = xb.shape
    return pl.pallas_call(
        _gla_kernel,
        grid=(bsz, GLA_HEADS),
        in_specs=[pl.BlockSpec((1, s, d), lambda b, h: (b, 0, 0)),
                  pl.BlockSpec((1, d, 4 * GLA_DK), lambda b, h: (h, 0, 0)),
                  pl.BlockSpec(wlr.shape, lambda b, h: (0, 0)),
                  pl.BlockSpec((1, LANES, GLA_DK), lambda b, h: (h, 0, 0)),
                  pl.BlockSpec((1, 1, GLA_DK), lambda b, h: (h, 0, 0)),
                  pl.BlockSpec(nw.shape, lambda b, h: (0, 0)),
                  pl.BlockSpec(tri.shape, lambda b, h: (0, 0))],
        out_specs=pl.BlockSpec((1, s, GLA_DV), lambda b, h: (b, 0, h)),
        out_shape=jax.ShapeDtypeStruct((bsz, s, GLA_HEADS * GLA_DV), BF16),
        scratch_shapes=[pltpu.VMEM((s, GLA_DK), F32)] * 6
        + [pltpu.VMEM((s, GLA_DK), BF16), pltpu.VMEM((s // GLA_CHUNK, GLA_DV, GLA_DK), F32),
           pltpu.VMEM((s // GLA_CHUNK, GLA_DK), F32)],
        compiler_params=pltpu.CompilerParams(dimension_semantics=("arbitrary", "arbitrary"),
                                             vmem_limit_bytes=VMEM_LIMIT),
        name="mixer_gla",
    )(xb, w, wlr, wgu, bg, nw, tri)


def _gdn_kernel(xb_ref, w_ref, wbd_ref, cw_ref, hp_ref, nw_ref, tri_ref, o_ref,
                q_s, k_s, v_s, beta_s, g_s, gate_s, u_s, wq_s, kd_s, attn_s, egl_s, st_s):
    s = xb_ref.shape[1]
    c = GDN_CHUNK
    dk = GDN_DK
    dv = GDN_DV
    nck = s // c
    row8 = lax.broadcasted_iota(jnp.int32, (SUBLANES, 3 * dk), 0)
    row = lax.broadcasted_iota(jnp.int32, (c, c), 0)
    col = lax.broadcasted_iota(jnp.int32, (c, c), 1)
    eye = (row == col).astype(F32)
    tri = tri_ref[...]
    avg = jnp.full((c, dk), 1.0 / dk, BF16)

    def head(h, carry):
        def proj(t, tail):
            r0 = pl.multiple_of(t * ROW_TILE, ROW_TILE)
            x = xb_ref[0, pl.ds(r0, ROW_TILE), :]
            p = jnp.dot(x, w_ref[h], preferred_element_type=F32)
            bd = jnp.dot(x, wbd_ref[h], preferred_element_type=F32)
            u = p[:, :3 * dk]
            qkv = _silu(_causal_conv(u, tail, cw_ref.at[h], row8))
            q = qkv[:, :dk]
            k = qkv[:, dk:2 * dk]
            q = q * lax.rsqrt(jnp.sum(q * q, axis=-1, keepdims=True) + NORM_EPS) * dk ** -0.5
            k = k * lax.rsqrt(jnp.sum(k * k, axis=-1, keepdims=True) + NORM_EPS)
            q_s[pl.ds(r0, ROW_TILE), :] = q
            k_s[pl.ds(r0, ROW_TILE), :] = k
            v_s[pl.ds(r0, ROW_TILE), :] = qkv[:, 2 * dk:]
            gate_s[h, pl.ds(r0, ROW_TILE), :] = p[:, 3 * dk:]
            beta = jax.nn.sigmoid(bd[:, 0:1])
            g = -jnp.exp(hp_ref[h, 0:1, :]) * jax.nn.softplus(bd[:, 1:2] + hp_ref[h, 1:2, :])
            beta_s[pl.ds(r0, ROW_TILE), :] = jnp.broadcast_to(beta, (ROW_TILE, dk))
            g_s[pl.ds(r0, ROW_TILE), :] = g
            return u[ROW_TILE - SUBLANES:, :]

        lax.fori_loop(0, s // ROW_TILE, proj, jnp.zeros((SUBLANES, 3 * dk), F32))

        def intra(grp, c1):
            js = range(CHUNK_UNROLL)
            ci = [grp * CHUNK_UNROLL + j for j in js]
            r0 = [pl.multiple_of(ci[j] * c, c) for j in js]
            q = [q_s[pl.ds(r0[j], c), :] for j in js]
            k = [k_s[pl.ds(r0[j], c), :] for j in js]
            v = [v_s[pl.ds(r0[j], c), :] for j in js]
            beta = [beta_s[pl.ds(r0[j], c), :] for j in js]
            gc = [_split_dot_nt_left(tri, g_s[pl.ds(r0[j], c), :]) for j in js]
            gc_row = [_split_dot_nt(avg, gc[j]) for j in js]
            dec = [jnp.exp(jnp.minimum(gc[j][:, :c] - gc_row[j], 0.0)) for j in js]
            kb = [k[j] * beta[j] for j in js]
            qk = [_bdot_nt(jnp.concatenate([kb[j], q[j]], axis=0), k[j]) for j in js]
            a = [jnp.where(col < row, qk[j][:c] * dec[j], 0.0) for j in js]
            attn = [jnp.where(col <= row, qk[j][c:] * dec[j], 0.0) for j in js]
            tinv = [eye - a[j] for j in js]
            ak = a
            n = 1
            while n < c // 2:
                ak = [_bdot(ak[j], ak[j]) for j in js]
                tinv = [tinv[j] + _bdot(tinv[j], ak[j]) for j in js]
                n *= 2
            egc = [jnp.exp(gc[j]) for j in js]
            uw = [_bdot(tinv[j], jnp.concatenate([v[j] * beta[j], kb[j] * egc[j]], axis=1))
                  for j in js]
            for j in js:
                gl = gc[j][c - 1:c, :]
                u_s[h, pl.ds(r0[j], c), :] = uw[j][:, :dv]
                wq_s[h, ci[j]] = jnp.concatenate([uw[j][:, dv:], q[j] * egc[j]], axis=0).astype(BF16)
                kd_s[h, pl.ds(r0[j], c), :] = (k[j] * jnp.exp(gl - gc[j])).astype(BF16)
                attn_s[h, pl.ds(r0[j], c), :] = attn[j].astype(BF16)
                egl_s[h, pl.ds(ci[j], 1), :] = jnp.exp(gl)
            return c1

        lax.fori_loop(0, nck // CHUNK_UNROLL, intra, 0)
        return carry

    lax.fori_loop(0, GDN_HEADS, head, 0)

    st_s[...] = jnp.zeros_like(st_s)

    def inter(ci, c2):
        r0 = pl.multiple_of(ci * c, c)
        hs = range(GDN_HEADS)
        tn = (((0,), (0,)), ((), ()))
        st = [st_s[h] for h in hs]
        ws_qs = [jnp.dot(wq_s[h, ci], st[h].astype(BF16), preferred_element_type=F32) for h in hs]
        v_new = [(u_s[h, pl.ds(r0, c), :] - ws_qs[h][:c]).astype(BF16) for h in hs]
        o = [ws_qs[h][c:] + jnp.dot(attn_s[h, pl.ds(r0, c), :], v_new[h], preferred_element_type=F32)
             for h in hs]
        kv = [lax.dot_general(kd_s[h, pl.ds(r0, c), :], v_new[h], tn, preferred_element_type=F32)
              for h in hs]
        for h in hs:
            st_s[h] = st[h] * egl_s[h, pl.ds(ci, 1), :] + kv[h]
            ms = jnp.mean(o[h] * o[h], axis=-1, keepdims=True)
            y = o[h] * lax.rsqrt(ms + NORM_EPS) * nw_ref[...] * _silu(gate_s[h, pl.ds(r0, c), :])
            o_ref[0, pl.ds(r0, c), h * dv:(h + 1) * dv] = y.astype(o_ref.dtype)
        return c2

    lax.fori_loop(0, nck, inter, 0)


def _gdn_call(xb, w, wbd, cw, hp, nw, tri):
    bsz, s, d = xb.shape
    nh = GDN_HEADS
    nck = s // GDN_CHUNK
    const = lambda a: pl.BlockSpec(a.shape, lambda b: (0,) * a.ndim)
    return pl.pallas_call(
        _gdn_kernel,
        grid=(bsz,),
        in_specs=[pl.BlockSpec((1, s, d), lambda b: (b, 0, 0)),
                  const(w), const(wbd), const(cw), const(hp), const(nw), const(tri)],
        out_specs=pl.BlockSpec((1, s, nh * GDN_DV), lambda b: (b, 0, 0)),
        out_shape=jax.ShapeDtypeStruct((bsz, s, nh * GDN_DV), BF16),
        scratch_shapes=[pltpu.VMEM((s, GDN_DK), F32)] * 5
        + [pltpu.VMEM((nh, s, GDN_DV), F32),
           pltpu.VMEM((nh, s, GDN_DV), F32),
           pltpu.VMEM((nh, nck, 2 * GDN_CHUNK, GDN_DK), BF16),
           pltpu.VMEM((nh, s, GDN_DK), BF16),
           pltpu.VMEM((nh, s, GDN_CHUNK), BF16),
           pltpu.VMEM((nh, nck, GDN_DK), F32),
           pltpu.VMEM((nh, GDN_DK, GDN_DV), F32)],
        compiler_params=pltpu.CompilerParams(dimension_semantics=("arbitrary",),
                                             vmem_limit_bytes=VMEM_LIMIT),
        name="mixer_gdn",
    )(xb, w, wbd, cw, hp, nw, tri)


def _merge_kernel(x_ref, xb_ref, ya_ref, yb_ref, yc_ref, yd_ref, wm_ref, wb_ref, wo_ref,
                  g_ref, b_ref, o_ref, ob_ref):
    xb = xb_ref[...]
    merged = None
    for n, y_ref in enumerate((ya_ref, yb_ref, yc_ref, yd_ref)):
        gate = jax.nn.sigmoid(jnp.dot(xb, wm_ref[:, n * D_MODEL:(n + 1) * D_MODEL],
                                      preferred_element_type=F32))
        term = gate * jnp.dot(y_ref[...], wb_ref[n], preferred_element_type=F32)
        merged = term if merged is None else merged + term
    h = _bdot(merged, wo_ref[...])
    y = _layer_norm(DN_ALPHA * x_ref[...] + h, g_ref[...], b_ref[...])
    o_ref[...] = y
    ob_ref[...] = y.astype(BF16)


def _merge_call(x, xb, ya, yb, yc, yd, wm, wb, wo, g, b):
    t, d = x.shape
    tm = 256
    tok = lambda width: pl.BlockSpec((tm, width), lambda i: (i, 0))
    const = lambda shape: pl.BlockSpec(shape, lambda i: (0,) * len(shape),
                                       pipeline_mode=pl.Buffered(1))
    return pl.pallas_call(
        _merge_kernel,
        grid=(t // tm,),
        in_specs=[tok(d), tok(d), tok(BRANCH_WIDTH), tok(BRANCH_WIDTH), tok(BRANCH_WIDTH),
                  tok(BRANCH_WIDTH), const(wm.shape), const(wb.shape), const(wo.shape),
                  const(g.shape), const(b.shape)],
        out_specs=[tok(d), tok(d)],
        out_shape=[jax.ShapeDtypeStruct((t, d), F32), jax.ShapeDtypeStruct((t, d), BF16)],
        compiler_params=pltpu.CompilerParams(dimension_semantics=("arbitrary",),
                                             vmem_limit_bytes=VMEM_LIMIT),
        name="merge",
    )(x, xb, ya, yb, yc, yd, wm, wb, wo, g, b)


def _route(scores, biased):
    s = [scores[:, e:e + 1] for e in range(N_EXPERTS)]
    b = [biased[:, e:e + 1] for e in range(N_EXPERTS)]
    gs = []
    for g in range(N_GROUPS):
        m = b[EXPERTS_PER_GROUP * g:EXPERTS_PER_GROUP * (g + 1)]
        best = None
        for i in range(EXPERTS_PER_GROUP):
            for j in range(i + 1, EXPERTS_PER_GROUP):
                ps = m[i] + m[j]
                best = ps if best is None else jnp.maximum(best, ps)
        gs.append(best)
    gidx = jnp.zeros_like(gs[0], dtype=jnp.int32)
    gval = gs[0]
    for g in range(1, N_GROUPS):
        take = gs[g] > gval
        gidx = jnp.where(take, g, gidx)
        gval = jnp.where(take, gs[g], gval)
    w = []
    for e in range(N_EXPERTS):
        g = e // EXPERTS_PER_GROUP
        beaten = jnp.zeros_like(gidx)
        for j in range(EXPERTS_PER_GROUP * g, EXPERTS_PER_GROUP * (g + 1)):
            if j == e:
                continue
            wins = (b[j] >= b[e]) if j < e else (b[j] > b[e])
            beaten = beaten + wins.astype(jnp.int32)
        sel = (gidx == g) & (beaten < 2)
        w.append(jnp.where(sel, s[e], 0.0))
    tot = w[0]
    for e in range(1, N_EXPERTS):
        tot = tot + w[e]
    lane = lax.broadcasted_iota(jnp.int32, scores.shape, 1)
    comb = jnp.zeros_like(scores)
    for e in range(N_EXPERTS):
        comb = jnp.where(lane == e, w[e] / tot, comb)
    return comb


def _moe_kernel(x_ref, xb_ref, wr_ref, rb_ref, wg_ref, wu_ref, wd_ref, g_ref, b_ref,
                o_ref, ob_ref, acc_ref, comb_ref):
    e = pl.program_id(1)

    @pl.when(e == 0)
    def _():
        logits = jnp.dot(x_ref[...], wr_ref[...], precision=lax.Precision.HIGHEST,
                         preferred_element_type=F32)
        scores = jax.nn.sigmoid(logits)
        comb_ref[...] = _route(scores, scores + rb_ref[...])
        acc_ref[...] = jnp.zeros_like(acc_ref)

    xb = xb_ref[...]
    hg = jnp.dot(xb, wg_ref[0], preferred_element_type=F32)
    hu = jnp.dot(xb, wu_ref[0], preferred_element_type=F32)
    y = _bdot(_silu(hg) * hu, wd_ref[0])
    lane = lax.broadcasted_iota(jnp.int32, comb_ref.shape, 1)
    ce = jnp.sum(jnp.where(lane == e, comb_ref[...], 0.0), axis=-1, keepdims=True)
    acc_ref[...] += ce * y

    @pl.when(e == N_EXPERTS - 1)
    def _():
        out = _layer_norm(DN_ALPHA * x_ref[...] + acc_ref[...], g_ref[...], b_ref[...])
        o_ref[...] = out
        ob_ref[...] = out.astype(BF16)


def _moe_call(x, xb, wr, rb, wg, wu, wd, g, b):
    t, d = x.shape
    tm = 512
    tok = lambda: pl.BlockSpec((tm, d), lambda i, e: (i, 0))
    const = lambda shape: pl.BlockSpec(shape, lambda i, e: (0,) * len(shape))
    return pl.pallas_call(
        _moe_kernel,
        grid=(t // tm, N_EXPERTS),
        in_specs=[tok(), tok(), const(wr.shape), const(rb.shape),
                  pl.BlockSpec((1, d, D_EXPERT), lambda i, e: (e, 0, 0)),
                  pl.BlockSpec((1, d, D_EXPERT), lambda i, e: (e, 0, 0)),
                  pl.BlockSpec((1, D_EXPERT, d), lambda i, e: (e, 0, 0)),
                  const(g.shape), const(b.shape)],
        out_specs=[tok(), tok()],
        out_shape=[jax.ShapeDtypeStruct((t, d), F32), jax.ShapeDtypeStruct((t, d), BF16)],
        scratch_shapes=[pltpu.VMEM((tm, d), F32), pltpu.VMEM((tm, LANES), F32)],
        compiler_params=pltpu.CompilerParams(dimension_semantics=("arbitrary", "arbitrary"),
                                             vmem_limit_bytes=VMEM_LIMIT),
        name="moe",
    )(x, xb, wr, rb, wg, wu, wd, g, b)


def _block_diag(w):
    g, n, _ = w.shape
    eye = jnp.eye(g, dtype=w.dtype)
    return (eye[:, None, :, None] * w[:, :, None, :]).reshape(g * n, g * n)


def _per_head(w, off, n_heads, width, parts):
    cols = [w[:, off + p * n_heads * width: off + (p + 1) * n_heads * width]
            .reshape(w.shape[0], n_heads, width) for p in parts]
    return jnp.transpose(jnp.concatenate(cols, axis=2), (1, 0, 2))


def _row(v):
    return v.reshape(1, -1).astype(F32)


def kernel(x, w_in, conv_a_w, conv_a_b, rg_w_a, rg_b_a, rg_w_x, rg_b_x, rg_lambda, gdn_conv_w, gdn_a_log, gdn_dt_bias, gdn_norm_w, gla_w_gate_up, gla_b_gate, gla_norm_w, w_branch, w_out, ln1_g, ln1_b, w_router, router_bias, w_gate, w_up, w_down, ln2_g, ln2_b):
    bsz, s, d = x.shape
    t = bsz * s
    c = GDN_CHUNK
    ii = jnp.arange(SB_BLOCK)
    u2 = jnp.concatenate([(ii[:, None] >= ii[None, :]).astype(BF16),
                          jnp.ones((SB_BLOCK, SB_BLOCK), BF16)], axis=1)
    jj = jnp.arange(c)
    tri = (jj[:, None] >= jj[None, :]).astype(BF16)
    wr = jnp.pad(w_router.astype(F32), ((0, 0), (0, LANES - N_EXPERTS)))
    rb = jnp.pad(router_bias.astype(F32), (0, LANES - N_EXPERTS)).reshape(1, LANES)

    xf = x.reshape(t, d)
    xb = xf.astype(BF16)
    for l in range(DEPTH):
        wl = w_in[l]
        wlb = wl.astype(BF16)
        xb3 = xb.reshape(bsz, s, d)

        wg = jnp.concatenate([_block_diag(rg_w_a[l]), _block_diag(rg_w_x[l])], axis=1).astype(BF16)
        bg = jnp.concatenate([rg_b_a[l], rg_b_x[l]]).reshape(1, -1).astype(F32)
        ya = _lru_call(xb3, wlb[:, _OFF_A:_OFF_A + LRU_WIDTH], conv_a_w[l].astype(F32),
                       _row(conv_a_b[l]), wg, bg, _row(rg_lambda[l]))

        w_gdn = jnp.concatenate([_per_head(wlb, _OFF_GDN_QKV, GDN_HEADS, GDN_DK, (0, 1, 2)),
                                 _per_head(wlb, _OFF_GDN_GATE, GDN_HEADS, GDN_DV, (0,))], axis=2)
        wbd = jnp.stack([wlb[:, _OFF_GDN_BETA:_OFF_GDN_BETA + GDN_HEADS],
                         wlb[:, _OFF_GDN_DECAY:_OFF_GDN_DECAY + GDN_HEADS]], axis=2)
        wbd = jnp.pad(jnp.transpose(wbd, (1, 0, 2)), ((0, 0), (0, 0), (0, LANES - 2)))
        cw_gdn = jnp.transpose(gdn_conv_w[l].astype(F32).reshape(CONV_WIDTH, 3, GDN_HEADS, GDN_DK),
                               (2, 0, 1, 3)).reshape(GDN_HEADS, CONV_WIDTH, 3 * GDN_DK)
        hp = jnp.zeros((GDN_HEADS, SUBLANES, GDN_DK), F32)
        hp = hp.at[:, 0, :].set(gdn_a_log[l].astype(F32)[:, None])
        hp = hp.at[:, 1, :].set(gdn_dt_bias[l].astype(F32)[:, None])
        yb = _gdn_call(xb3, w_gdn, wbd, cw_gdn, hp, _row(gdn_norm_w[l]), tri)

        yc = _sb_call(xb3, wlb[:, _OFF_SB:_OFF_SB + 3 * SB_HEADS * SB_DH], u2)

        q_off = _OFF_GLA_QKV
        w_gla = jnp.concatenate([
            _per_head(wlb, q_off, GLA_HEADS, GLA_DK, (0, 1)),
            _per_head(wlb, q_off + 2 * GLA_HEADS * GLA_DK, GLA_HEADS, GLA_DV, (0,)),
            _per_head(wlb, _OFF_GLA_GATE, GLA_HEADS, GLA_DV, (0,))], axis=2)
        wlr = jnp.pad(wlb[:, _OFF_GLA_LR:_OFF_GLA_LR + GLA_GATE_RANK],
                      ((0, 0), (0, LANES - GLA_GATE_RANK)))
        wgu = jnp.pad(jnp.transpose(gla_w_gate_up[l].reshape(GLA_GATE_RANK, GLA_HEADS, GLA_DK), (1, 0, 2)),
                      ((0, 0), (0, LANES - GLA_GATE_RANK), (0, 0))).astype(BF16)
        bgl = gla_b_gate[l].astype(F32).reshape(GLA_HEADS, 1, GLA_DK)
        yd = _gla_call(xb3, w_gla, wlr, wgu, bgl, _row(gla_norm_w[l]), tri)

        xf, xb = _merge_call(xf, xb, ya.reshape(t, -1), yb.reshape(t, -1), yc.reshape(t, -1),
                             yd.reshape(t, -1), wlb[:, _OFF_MERGE:], w_branch[l].astype(BF16),
                             w_out[l].astype(BF16), _row(ln1_g[l]), _row(ln1_b[l]))

        xf, xb = _moe_call(xf, xb, wr, rb, w_gate[l].astype(BF16), w_up[l].astype(BF16),
                           w_down[l].astype(BF16), _row(ln2_g[l]), _row(ln2_b[l]))
    return xf.reshape(bsz, s, d)
```

```python
import functools

import jax
import jax.numpy as jnp
from jax import lax
from jax.experimental import pallas as pl
from jax.experimental.pallas import tpu as pltpu

F32 = jnp.float32
BF16 = jnp.bfloat16

D_MODEL = 1024
DEPTH = 4
LRU_WIDTH = 512
LRU_BLOCKS = 8
LRU_C = 8.0
CONV_WIDTH = 4
GDN_HEADS = 4
GDN_DK = 128
GDN_DV = 128
GDN_CHUNK = 64
SB_HEADS = 8
SB_DH = 64
SB_BLOCK = 128
GLA_HEADS = 4
GLA_DK = 128
GLA_DV = 128
GLA_GATE_RANK = 16
GLA_TAU = 16.0
GLA_CHUNK = 64
N_BRANCHES = 4
BRANCH_WIDTH = 512
N_EXPERTS = 16
N_GROUPS = 4
EXPERTS_PER_GROUP = 4
D_EXPERT = 512
LN_EPS = 1e-5
NORM_EPS = 1e-6
DN_ALPHA = (2 * DEPTH) ** 0.25

_OFF_A = 0
_OFF_GDN_QKV = _OFF_A + LRU_WIDTH
_OFF_GDN_BETA = _OFF_GDN_QKV + 3 * GDN_HEADS * GDN_DK
_OFF_GDN_DECAY = _OFF_GDN_BETA + GDN_HEADS
_OFF_GDN_GATE = _OFF_GDN_DECAY + GDN_HEADS
_OFF_SB = _OFF_GDN_GATE + GDN_HEADS * GDN_DV
_OFF_GLA_QKV = _OFF_SB + 3 * SB_HEADS * SB_DH
_OFF_GLA_LR = _OFF_GLA_QKV + GLA_HEADS * (2 * GLA_DK + GLA_DV)
_OFF_GLA_GATE = _OFF_GLA_LR + GLA_GATE_RANK
_OFF_MERGE = _OFF_GLA_GATE + GLA_HEADS * GLA_DV

LANES = 128
SUBLANES = 8
ROW_TILE = 256
CHUNK_UNROLL = 8
VMEM_LIMIT = 56 * 1024 * 1024


def _bdot(a, b):
    return jnp.dot(a.astype(BF16), b.astype(BF16), preferred_element_type=F32)


def _bdot_nt(a, b):
    return lax.dot_general(a.astype(BF16), b.astype(BF16), (((1,), (1,)), ((), ())),
                           preferred_element_type=F32)


def _bdot_tn(a, b):
    return lax.dot_general(a.astype(BF16), b.astype(BF16), (((0,), (0,)), ((), ())),
                           preferred_element_type=F32)


def _split_dot(a, b_bf16):
    hi = a.astype(BF16)
    lo = (a - hi.astype(F32)).astype(BF16)
    return (jnp.dot(hi, b_bf16, preferred_element_type=F32)
            + jnp.dot(lo, b_bf16, preferred_element_type=F32))


def _split_dot_nt(a_bf16, b):
    hi = b.astype(BF16)
    lo = (b - hi.astype(F32)).astype(BF16)
    dn = (((1,), (1,)), ((), ()))
    return (lax.dot_general(a_bf16, hi, dn, preferred_element_type=F32)
            + lax.dot_general(a_bf16, lo, dn, preferred_element_type=F32))


def _silu(x):
    return x * jax.nn.sigmoid(x)


def _layer_norm(x, g, b):
    mu = jnp.mean(x, axis=-1, keepdims=True)
    xc = x - mu
    var = jnp.mean(xc * xc, axis=-1, keepdims=True)
    return xc * lax.rsqrt(var + LN_EPS) * g + b


def _shift_rows(u, tail, s, row8):
    ur = pltpu.roll(u, s, 0)
    top = jnp.where(row8 >= s, ur[:SUBLANES], pltpu.roll(tail, s, 0))
    return jnp.concatenate([top, ur[SUBLANES:]], axis=0)


def _causal_conv(u, tail, cw_ref, row8):
    y = u * cw_ref[CONV_WIDTH - 1:CONV_WIDTH, :]
    for s in range(1, CONV_WIDTH):
        k = CONV_WIDTH - 1 - s
        y = y + _shift_rows(u, tail, s, row8) * cw_ref[k:k + 1, :]
    return y


def _lru_kernel(xb_ref, w_ref, cw_ref, cb_ref, wg_ref, bg_ref, lam_ref, o_ref, tail_ref, h_ref):
    t = pl.program_id(1)
    tt = xb_ref.shape[1]
    w = LRU_WIDTH

    @pl.when(t == 0)
    def _():
        tail_ref[...] = jnp.zeros_like(tail_ref)
        h_ref[...] = jnp.zeros_like(h_ref)

    u = jnp.dot(xb_ref[0], w_ref[...], preferred_element_type=F32)
    row = lax.broadcasted_iota(jnp.int32, (tt, w), 0)
    row8 = lax.broadcasted_iota(jnp.int32, (SUBLANES, w), 0)
    xc = _causal_conv(u, tail_ref[...], cw_ref, row8) + cb_ref[...]
    tail_ref[...] = u[tt - SUBLANES:, :]

    gates = _bdot(xc, wg_ref[...]) + bg_ref[...]
    r = jax.nn.sigmoid(gates[:, :w])
    i = jax.nn.sigmoid(gates[:, w:])
    log_a = (-LRU_C) * r * jax.nn.softplus(-lam_ref[...])
    a = jnp.exp(log_a)
    mult = jnp.sqrt(-jnp.tanh(log_a) * (a * a + 1.0))
    mult = jnp.where(row + t * tt == 0, 1.0, mult)
    b = mult * i * xc

    d = 1
    while d < tt:
        keep = row >= d
        a_sh = jnp.where(keep, pltpu.roll(a, d, 0), 1.0)
        b_sh = jnp.where(keep, pltpu.roll(b, d, 0), 0.0)
        b = a * b_sh + b
        a = a * a_sh
        d *= 2
    h = a * h_ref[0:1, :] + b
    h_ref[...] = jnp.broadcast_to(h[tt - 1:tt, :], h_ref.shape)
    o_ref[0] = h.astype(o_ref.dtype)


def _lru_call(xb, w, cw, cb, wg, bg, lam):
    bsz, s, d = xb.shape
    tt = ROW_TILE
    const = lambda shape: pl.BlockSpec(shape, lambda b, t: (0,) * len(shape))
    return pl.pallas_call(
        _lru_kernel,
        grid=(bsz, s // tt),
        in_specs=[pl.BlockSpec((1, tt, d), lambda b, t: (b, t, 0)),
                  const(w.shape), const(cw.shape), const(cb.shape), const(wg.shape),
                  const(bg.shape), const(lam.shape)],
        out_specs=pl.BlockSpec((1, tt, LRU_WIDTH), lambda b, t: (b, t, 0)),
        out_shape=jax.ShapeDtypeStruct((bsz, s, LRU_WIDTH), BF16),
        scratch_shapes=[pltpu.VMEM((SUBLANES, LRU_WIDTH), F32), pltpu.VMEM((SUBLANES, LRU_WIDTH), F32)],
        compiler_params=pltpu.CompilerParams(dimension_semantics=("arbitrary", "arbitrary"),
                                             vmem_limit_bytes=VMEM_LIMIT),
        name="mixer_lru",
    )(xb, w, cw, cb, wg, bg, lam)


def _sb_kernel(xb_ref, w_ref, u2_ref, o_ref, q_s, k_s, v_s, acc_s, run_s):
    s = xb_ref.shape[1]
    blk = SB_BLOCK
    dh = SB_DH
    nh = SB_HEADS
    hd = nh * dh

    def proj(t, c):
        r0 = pl.multiple_of(t * ROW_TILE, ROW_TILE)
        qkv = jnp.dot(xb_ref[0, pl.ds(r0, ROW_TILE), :], w_ref[...], preferred_element_type=F32)
        for h in range(nh):
            q_s[h, pl.ds(r0, ROW_TILE), :] = (qkv[:, h * dh:(h + 1) * dh] * dh ** -0.5).astype(BF16)
            k_s[h, pl.ds(r0, ROW_TILE), :] = qkv[:, hd + h * dh:hd + (h + 1) * dh].astype(BF16)
            v_s[h, pl.ds(r0, ROW_TILE), :] = qkv[:, 2 * hd + h * dh:2 * hd + (h + 1) * dh].astype(BF16)
        return c

    lax.fori_loop(0, s // ROW_TILE, proj, 0)

    row = lax.broadcasted_iota(jnp.int32, (blk, blk), 0)
    col = lax.broadcasted_iota(jnp.int32, (blk, blk), 1)
    u2 = u2_ref[...]

    def qloop(qi, c):
        q0 = pl.multiple_of(qi * blk, blk)
        acc_s[...] = jnp.zeros_like(acc_s)
        run_s[...] = jnp.zeros_like(run_s)

        def block(k0, diagonal):
            hs = range(nh)
            nt = (((1,), (1,)), ((), ()))
            z = [lax.dot_general(q_s[h, pl.ds(q0, blk), :], k_s[h, pl.ds(k0, blk), :], nt,
                                 preferred_element_type=F32) for h in hs]
            nz = [-z[h] for h in hs]
            lk = [jnp.minimum(nz[h], 0.0) - jnp.log(1.0 + jnp.exp(jnp.minimum(z[h], nz[h]))) for h in hs]
            if diagonal:
                lk = [jnp.where(col < row, lk[h], 0.0) for h in hs]
            hi = [lk[h].astype(BF16) for h in hs]
            lo = [(lk[h] - hi[h].astype(F32)).astype(BF16) for h in hs]
            cs2 = [jnp.dot(jnp.concatenate([hi[h], lo[h]], axis=1), u2, preferred_element_type=F32)
                   for h in hs]
            p = [jnp.exp(z[h] + cs2[h][:, :blk] + run_s[h]) for h in hs]
            if diagonal:
                p = [jnp.where(col < row, p[h], 0.0) for h in hs]
            pv = [jnp.dot(p[h].astype(BF16), v_s[h, pl.ds(k0, blk), :], preferred_element_type=F32)
                  for h in hs]
            for h in hs:
                acc_s[h] += pv[h]
                run_s[h] += cs2[h][:, blk:]

        block(q0, True)

        def kloop(t, c2):
            block(pl.multiple_of((qi - t) * blk, blk), False)
            return c2

        lax.fori_loop(1, qi + 1, kloop, 0)
        for h in range(nh):
            o_ref[0, pl.ds(q0, blk), h * dh:(h + 1) * dh] = acc_s[h].astype(o_ref.dtype)
        return c

    lax.fori_loop(0, s // blk, qloop, 0)


def _sb_call(xb, w, u2):
    bsz, s, d = xb.shape
    return pl.pallas_call(
        _sb_kernel,
        grid=(bsz,),
        in_specs=[pl.BlockSpec((1, s, d), lambda b: (b, 0, 0)),
                  pl.BlockSpec(w.shape, lambda b: (0, 0)),
                  pl.BlockSpec(u2.shape, lambda b: (0, 0))],
        out_specs=pl.BlockSpec((1, s, SB_HEADS * SB_DH), lambda b: (b, 0, 0)),
        out_shape=jax.ShapeDtypeStruct((bsz, s, SB_HEADS * SB_DH), BF16),
        scratch_shapes=[pltpu.VMEM((SB_HEADS, s, SB_DH), BF16)] * 3
        + [pltpu.VMEM((SB_HEADS, SB_BLOCK, SB_DH), F32), pltpu.VMEM((SB_HEADS, SB_BLOCK, SB_BLOCK), F32)],
        compiler_params=pltpu.CompilerParams(dimension_semantics=("arbitrary",),
                                             vmem_limit_bytes=VMEM_LIMIT),
        name="mixer_sb",
    )(xb, w, u2)


def _gla_kernel(xb_ref, w_ref, wlr_ref, wgu_ref, bg_ref, nw_ref, tri_ref, o_ref,
                q_s, k_s, v_s, g_s, gate_s, oi_s, qd_s, m_s, eb_s):
    s = xb_ref.shape[1]
    c = GLA_CHUNK
    dk = GLA_DK
    nck = s // c

    def proj(t, carry):
        r0 = pl.multiple_of(t * ROW_TILE, ROW_TILE)
        x = xb_ref[0, pl.ds(r0, ROW_TILE), :]
        p = jnp.dot(x, w_ref[0], preferred_element_type=F32)
        lr = jnp.dot(x, wlr_ref[...], preferred_element_type=F32)
        gpre = _bdot(lr, wgu_ref[0]) + bg_ref[0]
        g_s[pl.ds(r0, ROW_TILE), :] = jax.nn.log_sigmoid(gpre) * (1.0 / GLA_TAU)
        q_s[pl.ds(r0, ROW_TILE), :] = p[:, :dk] * dk ** -0.5
        k_s[pl.ds(r0, ROW_TILE), :] = p[:, dk:2 * dk]
        v_s[pl.ds(r0, ROW_TILE), :] = p[:, 2 * dk:3 * dk]
        gate_s[pl.ds(r0, ROW_TILE), :] = p[:, 3 * dk:]
        return carry

    lax.fori_loop(0, s // ROW_TILE, proj, 0)

    row = lax.broadcasted_iota(jnp.int32, (c, c), 0)
    col = lax.broadcasted_iota(jnp.int32, (c, c), 1)
    tri = tri_ref[...]

    def intra(grp, carry):
        u = range(CHUNK_UNROLL)
        ci = [grp * CHUNK_UNROLL + j for j in u]
        r0 = [pl.multiple_of(ci[j] * c, c) for j in u]
        q = [q_s[pl.ds(r0[j], c), :] for j in u]
        k = [k_s[pl.ds(r0[j], c), :] for j in u]
        v = [v_s[pl.ds(r0[j], c), :] for j in u]
        bc = [_split_dot_nt_left(tri, g_s[pl.ds(r0[j], c), :]) for j in u]
        blast = [bc[j][c - 1:c, :] for j in u]
        qd = [q[j] * jnp.exp(bc[j]) for j in u]
        ki = [k[j] * jnp.exp(-bc[j]) for j in u]
        kd = [k[j] * jnp.exp(blast[j] - bc[j]) for j in u]
        attn = [jnp.where(col <= row, _bdot_nt(qd[j], ki[j]), 0.0) for j in u]
        oi = [_bdot(attn[j], v[j]) for j in u]
        m = [_bdot_tn(v[j], kd[j]) for j in u]
        for j in u:
            oi_s[pl.ds(r0[j], c), :] = oi[j]
            qd_s[pl.ds(r0[j], c), :] = qd[j].astype(BF16)
            m_s[ci[j]] = m[j]
            eb_s[pl.ds(ci[j], 1), :] = jnp.exp(blast[j])
        return carry

    lax.fori_loop(0, nck // CHUNK_UNROLL, intra, 0)

    def inter(grp, st):
        for j in range(CHUNK_UNROLL):
            ci = grp * CHUNK_UNROLL + j
            r0 = pl.multiple_of(ci * c, c)
            o = oi_s[pl.ds(r0, c), :] + lax.dot_general(
                qd_s[pl.ds(r0, c), :], st.astype(BF16), (((1,), (1,)), ((), ())),
                preferred_element_type=F32)
            st = st * eb_s[pl.ds(ci, 1), :] + m_s[ci]
            ms = jnp.mean(o * o, axis=-1, keepdims=True)
            y = o * lax.rsqrt(ms + NORM_EPS) * nw_ref[...] * _silu(gate_s[pl.ds(r0, c), :])
            o_ref[0, pl.ds(r0, c), :] = y.astype(o_ref.dtype)
        return st

    lax.fori_loop(0, nck // CHUNK_UNROLL, inter, jnp.zeros((GLA_DV, dk), F32))


def _split_dot_nt_left(tri_bf16, x):
    hi = x.astype(BF16)
    lo = (x - hi.astype(F32)).astype(BF16)
    return (jnp.dot(tri_bf16, hi, preferred_element_type=F32)
            + jnp.dot(tri_bf16, lo, preferred_element_type=F32))


def _gla_call(xb, w, wlr, wgu, bg, nw, tri):
    bsz, s, d = xb.shape
    return pl.pallas_call(
        _gla_kernel,
        grid=(bsz, GLA_HEADS),
        in_specs=[pl.BlockSpec((1, s, d), lambda b, h: (b, 0, 0)),
                  pl.BlockSpec((1, d, 4 * GLA_DK), lambda b, h: (h, 0, 0)),
                  pl.BlockSpec(wlr.shape, lambda b, h: (0, 0)),
                  pl.BlockSpec((1, LANES, GLA_DK), lambda b, h: (h, 0, 0)),
                  pl.BlockSpec((1, 1, GLA_DK), lambda b, h: (h, 0, 0)),
                  pl.BlockSpec(nw.shape, lambda b, h: (0, 0)),
                  pl.BlockSpec(tri.shape, lambda b, h: (0, 0))],
        out_specs=pl.BlockSpec((1, s, GLA_DV), lambda b, h: (b, 0, h)),
        out_shape=jax.ShapeDtypeStruct((bsz, s, GLA_HEADS * GLA_DV), BF16),
        scratch_shapes=[pltpu.VMEM((s, GLA_DK), F32)] * 6
        + [pltpu.VMEM((s, GLA_DK), BF16), pltpu.VMEM((s // GLA_CHUNK, GLA_DV, GLA_DK), F32),
           pltpu.VMEM((s // GLA_CHUNK, GLA_DK), F32)],
        compiler_params=pltpu.CompilerParams(dimension_semantics=("arbitrary", "arbitrary"),
                                             vmem_limit_bytes=VMEM_LIMIT),
        name="mixer_gla",
    )(xb, w, wlr, wgu, bg, nw, tri)


def _gdn_kernel(xb_ref, w_ref, wbd_ref, cw_ref, hp_ref, nw_ref, tri_ref, o_ref,
                q_s, k_s, v_s, beta_s, g_s, gate_s, u_s, wq_s, kd_s, attn_s, egl_s, st_s):
    s = xb_ref.shape[1]
    c = GDN_CHUNK
    dk = GDN_DK
    dv = GDN_DV
    nck = s // c
    row8 = lax.broadcasted_iota(jnp.int32, (SUBLANES, 3 * dk), 0)
    row = lax.broadcasted_iota(jnp.int32, (c, c), 0)
    col = lax.broadcasted_iota(jnp.int32, (c, c), 1)
    eye = (row == col).astype(F32)
    tri = tri_ref[...]
    avg = jnp.full((c, dk), 1.0 / dk, BF16)

    def head(h, carry):
        def proj(t, tail):
            r0 = pl.multiple_of(t * ROW_TILE, ROW_TILE)
            x = xb_ref[0, pl.ds(r0, ROW_TILE), :]
            p = jnp.dot(x, w_ref[h], preferred_element_type=F32)
            bd = jnp.dot(x, wbd_ref[h], preferred_element_type=F32)
            u = p[:, :3 * dk]
            qkv = _silu(_causal_conv(u, tail, cw_ref.at[h], row8))
            q = qkv[:, :dk]
            k = qkv[:, dk:2 * dk]
            q = q * lax.rsqrt(jnp.sum(q * q, axis=-1, keepdims=True) + NORM_EPS) * dk ** -0.5
            k = k * lax.rsqrt(jnp.sum(k * k, axis=-1, keepdims=True) + NORM_EPS)
            q_s[pl.ds(r0, ROW_TILE), :] = q
            k_s[pl.ds(r0, ROW_TILE), :] = k
            v_s[pl.ds(r0, ROW_TILE), :] = qkv[:, 2 * dk:]
            gate_s[h, pl.ds(r0, ROW_TILE), :] = p[:, 3 * dk:]
            beta = jax.nn.sigmoid(bd[:, 0:1])
            g = -jnp.exp(hp_ref[h, 0:1, :]) * jax.nn.softplus(bd[:, 1:2] + hp_ref[h, 1:2, :])
            beta_s[pl.ds(r0, ROW_TILE), :] = jnp.broadcast_to(beta, (ROW_TILE, dk))
            g_s[pl.ds(r0, ROW_TILE), :] = g
            return u[ROW_TILE - SUBLANES:, :]

        lax.fori_loop(0, s // ROW_TILE, proj, jnp.zeros((SUBLANES, 3 * dk), F32))

        def intra(grp, c1):
            js = range(CHUNK_UNROLL)
            ci = [grp * CHUNK_UNROLL + j for j in js]
            r0 = [pl.multiple_of(ci[j] * c, c) for j in js]
            q = [q_s[pl.ds(r0[j], c), :] for j in js]
            k = [k_s[pl.ds(r0[j], c), :] for j in js]
            v = [v_s[pl.ds(r0[j], c), :] for j in js]
            beta = [beta_s[pl.ds(r0[j], c), :] for j in js]
            gc = [_split_dot_nt_left(tri, g_s[pl.ds(r0[j], c), :]) for j in js]
            gc_row = [_split_dot_nt(avg, gc[j]) for j in js]
            dec = [jnp.exp(jnp.minimum(gc[j][:, :c] - gc_row[j], 0.0)) for j in js]
            kb = [k[j] * beta[j] for j in js]
            qk = [_bdot_nt(jnp.concatenate([kb[j], q[j]], axis=0), k[j]) for j in js]
            a = [jnp.where(col < row, qk[j][:c] * dec[j], 0.0) for j in js]
            attn = [jnp.where(col <= row, qk[j][c:] * dec[j], 0.0) for j in js]
            tinv = [eye - a[j] for j in js]
            ak = a
            n = 1
            while n < c // 2:
                ak = [_bdot(ak[j], ak[j]) for j in js]
                tinv = [tinv[j] + _bdot(tinv[j], ak[j]) for j in js]
                n *= 2
            egc = [jnp.exp(gc[j]) for j in js]
            uw = [_bdot(tinv[j], jnp.concatenate([v[j] * beta[j], kb[j] * egc[j]], axis=1))
                  for j in js]
            for j in js:
                gl = gc[j][c - 1:c, :]
                u_s[h, pl.ds(r0[j], c), :] = uw[j][:, :dv]
                wq_s[h, ci[j]] = jnp.concatenate([uw[j][:, dv:], q[j] * egc[j]], axis=0).astype(BF16)
                kd_s[h, pl.ds(r0[j], c), :] = (k[j] * jnp.exp(gl - gc[j])).astype(BF16)
                attn_s[h, pl.ds(r0[j], c), :] = attn[j].astype(BF16)
                egl_s[h, pl.ds(ci[j], 1), :] = jnp.exp(gl)
            return c1

        lax.fori_loop(0, nck // CHUNK_UNROLL, intra, 0)
        return carry

    lax.fori_loop(0, GDN_HEADS, head, 0)

    st_s[...] = jnp.zeros_like(st_s)

    def inter(ci, c2):
        r0 = pl.multiple_of(ci * c, c)
        hs = range(GDN_HEADS)
        tn = (((0,), (0,)), ((), ()))
        st = [st_s[h] for h in hs]
        ws_qs = [jnp.dot(wq_s[h, ci], st[h].astype(BF16), preferred_element_type=F32) for h in hs]
        v_new = [(u_s[h, pl.ds(r0, c), :] - ws_qs[h][:c]).astype(BF16) for h in hs]
        o = [ws_qs[h][c:] + jnp.dot(attn_s[h, pl.ds(r0, c), :], v_new[h], preferred_element_type=F32)
             for h in hs]
        kv = [lax.dot_general(kd_s[h, pl.ds(r0, c), :], v_new[h], tn, preferred_element_type=F32)
              for h in hs]
        for h in hs:
            st_s[h] = st[h] * egl_s[h, pl.ds(ci, 1), :] + kv[h]
            ms = jnp.mean(o[h] * o[h], axis=-1, keepdims=True)
            y = o[h] * lax.rsqrt(ms + NORM_EPS) * nw_ref[...] * _silu(gate_s[h, pl.ds(r0, c), :])
            o_ref[0, pl.ds(r0, c), h * dv:(h + 1) * dv] = y.astype(o_ref.dtype)
        return c2

    lax.fori_loop(0, nck, inter, 0)


def _gdn_call(xb, w, wbd, cw, hp, nw, tri):
    bsz, s, d = xb.shape
    nh = GDN_HEADS
    nck = s // GDN_CHUNK
    const = lambda a: pl.BlockSpec(a.shape, lambda b: (0,) * a.ndim)
    return pl.pallas_call(
        _gdn_kernel,
        grid=(bsz,),
        in_specs=[pl.BlockSpec((1, s, d), lambda b: (b, 0, 0)),
                  const(w), const(wbd), const(cw), const(hp), const(nw), const(tri)],
        out_specs=pl.BlockSpec((1, s, nh * GDN_DV), lambda b: (b, 0, 0)),
        out_shape=jax.ShapeDtypeStruct((bsz, s, nh * GDN_DV), BF16),
        scratch_shapes=[pltpu.VMEM((s, GDN_DK), F32)] * 5
        + [pltpu.VMEM((nh, s, GDN_DV), F32),
           pltpu.VMEM((nh, s, GDN_DV), F32),
           pltpu.VMEM((nh, nck, 2 * GDN_CHUNK, GDN_DK), BF16),
           pltpu.VMEM((nh, s, GDN_DK), BF16),
           pltpu.VMEM((nh, s, GDN_CHUNK), BF16),
           pltpu.VMEM((nh, nck, GDN_DK), F32),
           pltpu.VMEM((nh, GDN_DK, GDN_DV), F32)],
        compiler_params=pltpu.CompilerParams(dimension_semantics=("arbitrary",),
                                             vmem_limit_bytes=VMEM_LIMIT),
        name="mixer_gdn",
    )(xb, w, wbd, cw, hp, nw, tri)


def _merge_kernel(x_ref, xb_ref, ya_ref, yb_ref, yc_ref, yd_ref, wm_ref, wb_ref, wo_ref,
                  g_ref, b_ref, o_ref, ob_ref):
    xb = xb_ref[...]
    merged = None
    for n, y_ref in enumerate((ya_ref, yb_ref, yc_ref, yd_ref)):
        gate = jax.nn.sigmoid(jnp.dot(xb, wm_ref[:, n * D_MODEL:(n + 1) * D_MODEL],
                                      preferred_element_type=F32))
        term = gate * jnp.dot(y_ref[...], wb_ref[n], preferred_element_type=F32)
        merged = term if merged is None else merged + term
    h = _bdot(merged, wo_ref[...])
    y = _layer_norm(DN_ALPHA * x_ref[...] + h, g_ref[...], b_ref[...])
    o_ref[...] = y
    ob_ref[...] = y.astype(BF16)


def _merge_call(x, xb, ya, yb, yc, yd, wm, wb, wo, g, b):
    t, d = x.shape
    tm = 256
    tok = lambda width: pl.BlockSpec((tm, width), lambda i: (i, 0))
    const = lambda shape: pl.BlockSpec(shape, lambda i: (0,) * len(shape),
                                       pipeline_mode=pl.Buffered(1))
    return pl.pallas_call(
        _merge_kernel,
        grid=(t // tm,),
        in_specs=[tok(d), tok(d), tok(BRANCH_WIDTH), tok(BRANCH_WIDTH), tok(BRANCH_WIDTH),
                  tok(BRANCH_WIDTH), const(wm.shape), const(wb.shape), const(wo.shape),
                  const(g.shape), const(b.shape)],
        out_specs=[tok(d), tok(d)],
        out_shape=[jax.ShapeDtypeStruct((t, d), F32), jax.ShapeDtypeStruct((t, d), BF16)],
        compiler_params=pltpu.CompilerParams(dimension_semantics=("arbitrary",),
                                             vmem_limit_bytes=VMEM_LIMIT),
        name="merge",
    )(x, xb, ya, yb, yc, yd, wm, wb, wo, g, b)


def _route(scores, biased):
    s = [scores[e:e + 1, :] for e in range(N_EXPERTS)]
    b = [biased[e:e + 1, :] for e in range(N_EXPERTS)]
    gs = []
    for g in range(N_GROUPS):
        m = b[EXPERTS_PER_GROUP * g:EXPERTS_PER_GROUP * (g + 1)]
        best = None
        for i in range(EXPERTS_PER_GROUP):
            for j in range(i + 1, EXPERTS_PER_GROUP):
                ps = m[i] + m[j]
                best = ps if best is None else jnp.maximum(best, ps)
        gs.append(best)
    gidx = jnp.zeros_like(gs[0], dtype=jnp.int32)
    gval = gs[0]
    for g in range(1, N_GROUPS):
        take = gs[g] > gval
        gidx = jnp.where(take, g, gidx)
        gval = jnp.where(take, gs[g], gval)
    w = []
    for e in range(N_EXPERTS):
        g = e // EXPERTS_PER_GROUP
        beaten = jnp.zeros_like(gidx)
        for j in range(EXPERTS_PER_GROUP * g, EXPERTS_PER_GROUP * (g + 1)):
            if j == e:
                continue
            wins = (b[j] >= b[e]) if j < e else (b[j] > b[e])
            beaten = beaten + wins.astype(jnp.int32)
        sel = (gidx == g) & (beaten < 2)
        w.append(jnp.where(sel, s[e], 0.0))
    tot = w[0]
    for e in range(1, N_EXPERTS):
        tot = tot + w[e]
    return [w[e] / tot for e in range(N_EXPERTS)]


def _moe_kernel(x_ref, xb_ref, wr_ref, rb_ref, wgu_ref, wd_ref, g_ref, b_ref,
                o_ref, ob_ref, acc_ref, comb_ref, combt_ref):
    e = pl.program_id(1)

    @pl.when(e == 0)
    def _():
        logits = lax.dot_general(wr_ref[...], x_ref[...], (((1,), (1,)), ((), ())),
                                 precision=lax.Precision.HIGHEST, preferred_element_type=F32)
        scores = jax.nn.sigmoid(logits)
        rows = _route(scores, scores + rb_ref[...])
        combt_ref[...] = jnp.zeros_like(combt_ref)
        for k in range(N_EXPERTS):
            combt_ref[k:k + 1, :] = rows[k]
        comb_ref[...] = combt_ref[...].T
        acc_ref[...] = jnp.zeros_like(acc_ref)

    hgu = jnp.dot(xb_ref[...], wgu_ref[0], preferred_element_type=F32)
    y = _bdot(_silu(hgu[:, :D_EXPERT]) * hgu[:, D_EXPERT:], wd_ref[0])
    lane = lax.broadcasted_iota(jnp.int32, comb_ref.shape, 1)
    ce = jnp.sum(jnp.where(lane == e, comb_ref[...], 0.0), axis=-1, keepdims=True)
    acc_ref[...] += ce * y

    @pl.when(e == N_EXPERTS - 1)
    def _():
        out = _layer_norm(DN_ALPHA * x_ref[...] + acc_ref[...], g_ref[...], b_ref[...])
        o_ref[...] = out
        ob_ref[...] = out.astype(BF16)


def _moe_call(x, xb, wr, rb, wgu, wd, g, b):
    t, d = x.shape
    tm = 512
    tok = lambda: pl.BlockSpec((tm, d), lambda i, e: (i, 0))
    const = lambda shape: pl.BlockSpec(shape, lambda i, e: (0,) * len(shape))
    return pl.pallas_call(
        _moe_kernel,
        grid=(t // tm, N_EXPERTS),
        in_specs=[tok(), tok(), const(wr.shape), const(rb.shape),
                  pl.BlockSpec((1, d, 2 * D_EXPERT), lambda i, e: (e, 0, 0)),
                  pl.BlockSpec((1, D_EXPERT, d), lambda i, e: (e, 0, 0)),
                  const(g.shape), const(b.shape)],
        out_specs=[tok(), tok()],
        out_shape=[jax.ShapeDtypeStruct((t, d), F32), jax.ShapeDtypeStruct((t, d), BF16)],
        scratch_shapes=[pltpu.VMEM((tm, d), F32), pltpu.VMEM((tm, LANES), F32),
                        pltpu.VMEM((LANES, tm), F32)],
        compiler_params=pltpu.CompilerParams(dimension_semantics=("arbitrary", "arbitrary"),
                                             vmem_limit_bytes=VMEM_LIMIT),
        name="moe",
    )(x, xb, wr, rb, wgu, wd, g, b)


def _block_diag(w):
    g, n, _ = w.shape
    eye = jnp.eye(g, dtype=w.dtype)
    return (eye[:, None, :, None] * w[:, :, None, :]).reshape(g * n, g * n)


def _per_head(w, off, n_heads, width, parts):
    cols = [w[:, off + p * n_heads * width: off + (p + 1) * n_heads * width]
            .reshape(w.shape[0], n_heads, width) for p in parts]
    return jnp.transpose(jnp.concatenate(cols, axis=2), (1, 0, 2))


def _row(v):
    return v.reshape(1, -1).astype(F32)


def kernel(x, w_in, conv_a_w, conv_a_b, rg_w_a, rg_b_a, rg_w_x, rg_b_x, rg_lambda, gdn_conv_w, gdn_a_log, gdn_dt_bias, gdn_norm_w, gla_w_gate_up, gla_b_gate, gla_norm_w, w_branch, w_out, ln1_g, ln1_b, w_router, router_bias, w_gate, w_up, w_down, ln2_g, ln2_b):
    bsz, s, d = x.shape
    t = bsz * s
    c = GDN_CHUNK
    ii = jnp.arange(SB_BLOCK)
    u2 = jnp.concatenate([(ii[:, None] >= ii[None, :]).astype(BF16),
                          jnp.ones((SB_BLOCK, SB_BLOCK), BF16)], axis=1)
    u2 = jnp.concatenate([u2, u2], axis=0)
    jj = jnp.arange(c)
    tri = (jj[:, None] >= jj[None, :]).astype(BF16)
    wr = w_router.astype(F32).T
    rb = router_bias.astype(F32).reshape(N_EXPERTS, 1)

    xf = x.reshape(t, d)
    xb = xf.astype(BF16)
    for l in range(DEPTH):
        wl = w_in[l]
        wlb = wl.astype(BF16)
        xb3 = xb.reshape(bsz, s, d)

        wg = jnp.concatenate([_block_diag(rg_w_a[l]), _block_diag(rg_w_x[l])], axis=1).astype(BF16)
        bg = jnp.concatenate([rg_b_a[l], rg_b_x[l]]).reshape(1, -1).astype(F32)
        ya = _lru_call(xb3, wlb[:, _OFF_A:_OFF_A + LRU_WIDTH], conv_a_w[l].astype(F32),
                       _row(conv_a_b[l]), wg, bg, _row(rg_lambda[l]))

        w_gdn = jnp.concatenate([_per_head(wlb, _OFF_GDN_QKV, GDN_HEADS, GDN_DK, (0, 1, 2)),
                                 _per_head(wlb, _OFF_GDN_GATE, GDN_HEADS, GDN_DV, (0,))], axis=2)
        wbd = jnp.stack([wlb[:, _OFF_GDN_BETA:_OFF_GDN_BETA + GDN_HEADS],
                         wlb[:, _OFF_GDN_DECAY:_OFF_GDN_DECAY + GDN_HEADS]], axis=2)
        wbd = jnp.pad(jnp.transpose(wbd, (1, 0, 2)), ((0, 0), (0, 0), (0, LANES - 2)))
        cw_gdn = jnp.transpose(gdn_conv_w[l].astype(F32).reshape(CONV_WIDTH, 3, GDN_HEADS, GDN_DK),
                               (2, 0, 1, 3)).reshape(GDN_HEADS, CONV_WIDTH, 3 * GDN_DK)
        hp = jnp.zeros((GDN_HEADS, SUBLANES, GDN_DK), F32)
        hp = hp.at[:, 0, :].set(gdn_a_log[l].astype(F32)[:, None])
        hp = hp.at[:, 1, :].set(gdn_dt_bias[l].astype(F32)[:, None])
        yb = _gdn_call(xb3, w_gdn, wbd, cw_gdn, hp, _row(gdn_norm_w[l]), tri)

        yc = _sb_call(xb3, wlb[:, _OFF_SB:_OFF_SB + 3 * SB_HEADS * SB_DH], u2)

        q_off = _OFF_GLA_QKV
        w_gla = jnp.concatenate([
            _per_head(wlb, q_off, GLA_HEADS, GLA_DK, (0, 1)),
            _per_head(wlb, q_off + 2 * GLA_HEADS * GLA_DK, GLA_HEADS, GLA_DV, (0,)),
            _per_head(wlb, _OFF_GLA_GATE, GLA_HEADS, GLA_DV, (0,))], axis=2)
        wlr = jnp.pad(wlb[:, _OFF_GLA_LR:_OFF_GLA_LR + GLA_GATE_RANK],
                      ((0, 0), (0, LANES - GLA_GATE_RANK)))
        wgu = jnp.pad(jnp.transpose(gla_w_gate_up[l].reshape(GLA_GATE_RANK, GLA_HEADS, GLA_DK), (1, 0, 2)),
                      ((0, 0), (0, LANES - GLA_GATE_RANK), (0, 0))).astype(BF16)
        bgl = gla_b_gate[l].astype(F32).reshape(GLA_HEADS, 1, GLA_DK)
        yd = _gla_call(xb3, w_gla, wlr, wgu, bgl, _row(gla_norm_w[l]), tri)

        xf, xb = _merge_call(xf, xb, ya.reshape(t, -1), yb.reshape(t, -1), yc.reshape(t, -1),
                             yd.reshape(t, -1), wlb[:, _OFF_MERGE:], w_branch[l].astype(BF16),
                             w_out[l].astype(BF16), _row(ln1_g[l]), _row(ln1_b[l]))

        w_gu = jnp.concatenate([w_gate[l].astype(BF16), w_up[l].astype(BF16)], axis=2)
        xf, xb = _moe_call(xf, xb, wr, rb, w_gu, w_down[l].astype(BF16), _row(ln2_g[l]), _row(ln2_b[l]))
    return xf.reshape(bsz, s, d)
```

```python
import functools

import jax
import jax.numpy as jnp
from jax import lax
from jax.experimental import pallas as pl
from jax.experimental.pallas import tpu as pltpu

F32 = jnp.float32
BF16 = jnp.bfloat16

D_MODEL = 1024
DEPTH = 4
LRU_WIDTH = 512
LRU_BLOCKS = 8
LRU_C = 8.0
CONV_WIDTH = 4
GDN_HEADS = 4
GDN_DK = 128
GDN_DV = 128
GDN_CHUNK = 64
SB_HEADS = 8
SB_DH = 64
SB_BLOCK = 128
GLA_HEADS = 4
GLA_DK = 128
GLA_DV = 128
GLA_GATE_RANK = 16
GLA_TAU = 16.0
GLA_CHUNK = 64
N_BRANCHES = 4
BRANCH_WIDTH = 512
N_EXPERTS = 16
N_GROUPS = 4
EXPERTS_PER_GROUP = 4
D_EXPERT = 512
LN_EPS = 1e-5
NORM_EPS = 1e-6
DN_ALPHA = (2 * DEPTH) ** 0.25
LOG2E = 1.4426950408889634

_OFF_A = 0
_OFF_GDN_QKV = _OFF_A + LRU_WIDTH
_OFF_GDN_BETA = _OFF_GDN_QKV + 3 * GDN_HEADS * GDN_DK
_OFF_GDN_DECAY = _OFF_GDN_BETA + GDN_HEADS
_OFF_GDN_GATE = _OFF_GDN_DECAY + GDN_HEADS
_OFF_SB = _OFF_GDN_GATE + GDN_HEADS * GDN_DV
_OFF_GLA_QKV = _OFF_SB + 3 * SB_HEADS * SB_DH
_OFF_GLA_LR = _OFF_GLA_QKV + GLA_HEADS * (2 * GLA_DK + GLA_DV)
_OFF_GLA_GATE = _OFF_GLA_LR + GLA_GATE_RANK
_OFF_MERGE = _OFF_GLA_GATE + GLA_HEADS * GLA_DV

LANES = 128
SUBLANES = 8
ROW_TILE = 256
CHUNK_UNROLL = 8
MOE_TILE = 1024
MOE_CHUNK = 256
VMEM_LIMIT = 56 * 1024 * 1024


def _bdot(a, b):
    return jnp.dot(a.astype(BF16), b.astype(BF16), preferred_element_type=F32)


def _bdot_nt(a, b):
    return lax.dot_general(a.astype(BF16), b.astype(BF16), (((1,), (1,)), ((), ())),
                           preferred_element_type=F32)


def _bdot_tn(a, b):
    return lax.dot_general(a.astype(BF16), b.astype(BF16), (((0,), (0,)), ((), ())),
                           preferred_element_type=F32)


def _split_dot(a, b_bf16):
    hi = a.astype(BF16)
    lo = (a - hi.astype(F32)).astype(BF16)
    return (jnp.dot(hi, b_bf16, preferred_element_type=F32)
            + jnp.dot(lo, b_bf16, preferred_element_type=F32))


def _split_dot_nt(a_bf16, b):
    hi = b.astype(BF16)
    lo = (b - hi.astype(F32)).astype(BF16)
    dn = (((1,), (1,)), ((), ()))
    return (lax.dot_general(a_bf16, hi, dn, preferred_element_type=F32)
            + lax.dot_general(a_bf16, lo, dn, preferred_element_type=F32))


def _silu(x):
    return x * jax.nn.sigmoid(x)


def _layer_norm(x, g, b):
    mu = jnp.mean(x, axis=-1, keepdims=True)
    xc = x - mu
    var = jnp.mean(xc * xc, axis=-1, keepdims=True)
    return xc * lax.rsqrt(var + LN_EPS) * g + b


def _shift_rows(u, tail, s, row8):
    ur = pltpu.roll(u, s, 0)
    top = jnp.where(row8 >= s, ur[:SUBLANES], pltpu.roll(tail, s, 0))
    return jnp.concatenate([top, ur[SUBLANES:]], axis=0)


def _causal_conv(u, tail, cw_ref, row8):
    y = u * cw_ref[CONV_WIDTH - 1:CONV_WIDTH, :]
    for s in range(1, CONV_WIDTH):
        k = CONV_WIDTH - 1 - s
        y = y + _shift_rows(u, tail, s, row8) * cw_ref[k:k + 1, :]
    return y


def _lru_kernel(xb_ref, w_ref, cw_ref, cb_ref, wg_ref, bg_ref, lam_ref, o_ref, tail_ref, h_ref):
    t = pl.program_id(1)
    tt = xb_ref.shape[1]
    w = LRU_WIDTH

    @pl.when(t == 0)
    def _():
        tail_ref[...] = jnp.zeros_like(tail_ref)
        h_ref[...] = jnp.zeros_like(h_ref)

    u = jnp.dot(xb_ref[0], w_ref[...], preferred_element_type=F32)
    row = lax.broadcasted_iota(jnp.int32, (tt, w), 0)
    row8 = lax.broadcasted_iota(jnp.int32, (SUBLANES, w), 0)
    xc = _causal_conv(u, tail_ref[...], cw_ref, row8) + cb_ref[...]
    tail_ref[...] = u[tt - SUBLANES:, :]

    gates = _bdot(xc, wg_ref[...]) + bg_ref[...]
    r = jax.nn.sigmoid(gates[:, :w])
    i = jax.nn.sigmoid(gates[:, w:])
    log_a = (-LRU_C) * r * jax.nn.softplus(-lam_ref[...])
    a = jnp.exp(log_a)
    mult = jnp.sqrt(-jnp.tanh(log_a) * (a * a + 1.0))
    mult = jnp.where(row + t * tt == 0, 1.0, mult)
    b = mult * i * xc

    d = 1
    while d < tt:
        keep = row >= d
        a_sh = jnp.where(keep, pltpu.roll(a, d, 0), 1.0)
        b_sh = jnp.where(keep, pltpu.roll(b, d, 0), 0.0)
        b = a * b_sh + b
        a = a * a_sh
        d *= 2
    h = a * h_ref[0:1, :] + b
    h_ref[...] = jnp.broadcast_to(h[tt - 1:tt, :], h_ref.shape)
    o_ref[0] = h.astype(o_ref.dtype)


def _lru_call(xb, w, cw, cb, wg, bg, lam):
    bsz, s, d = xb.shape
    tt = ROW_TILE
    const = lambda shape: pl.BlockSpec(shape, lambda b, t: (0,) * len(shape))
    return pl.pallas_call(
        _lru_kernel,
        grid=(bsz, s // tt),
        in_specs=[pl.BlockSpec((1, tt, d), lambda b, t: (b, t, 0)),
                  const(w.shape), const(cw.shape), const(cb.shape), const(wg.shape),
                  const(bg.shape), const(lam.shape)],
        out_specs=pl.BlockSpec((1, tt, LRU_WIDTH), lambda b, t: (b, t, 0)),
        out_shape=jax.ShapeDtypeStruct((bsz, s, LRU_WIDTH), BF16),
        scratch_shapes=[pltpu.VMEM((SUBLANES, LRU_WIDTH), F32), pltpu.VMEM((SUBLANES, LRU_WIDTH), F32)],
        compiler_params=pltpu.CompilerParams(dimension_semantics=("arbitrary", "arbitrary"),
                                             vmem_limit_bytes=VMEM_LIMIT),
        name="mixer_lru",
    )(xb, w, cw, cb, wg, bg, lam)


def _sb_kernel(xb_ref, w_ref, u2_ref, o_ref, q_s, k_s, v_s, acc_s, run_s):
    s = xb_ref.shape[1]
    blk = SB_BLOCK
    dh = SB_DH
    nh = SB_HEADS
    hd = nh * dh

    def proj(t, c):
        r0 = pl.multiple_of(t * ROW_TILE, ROW_TILE)
        qkv = jnp.dot(xb_ref[0, pl.ds(r0, ROW_TILE), :], w_ref[...], preferred_element_type=F32)
        for h in range(nh):
            q_s[h, pl.ds(r0, ROW_TILE), :] = (qkv[:, h * dh:(h + 1) * dh] * (dh ** -0.5 * LOG2E)).astype(BF16)
            k_s[h, pl.ds(r0, ROW_TILE), :] = qkv[:, hd + h * dh:hd + (h + 1) * dh].astype(BF16)
            v_s[h, pl.ds(r0, ROW_TILE), :] = qkv[:, 2 * hd + h * dh:2 * hd + (h + 1) * dh].astype(BF16)
        return c

    lax.fori_loop(0, s // ROW_TILE, proj, 0)

    row = lax.broadcasted_iota(jnp.int32, (blk, blk), 0)
    col = lax.broadcasted_iota(jnp.int32, (blk, blk), 1)
    u2 = u2_ref[...]

    def qloop(qi, c):
        q0 = pl.multiple_of(qi * blk, blk)
        acc_s[...] = jnp.zeros_like(acc_s)
        run_s[...] = jnp.zeros_like(run_s)

        def blocks(k0s, diagonal):
            ch = [(h, j) for j in range(len(k0s)) for h in range(nh)]
            nt = (((1,), (1,)), ((), ()))
            z = {c_: lax.dot_general(q_s[c_[0], pl.ds(q0, blk), :], k_s[c_[0], pl.ds(k0s[c_[1]], blk), :],
                                     nt, preferred_element_type=F32) for c_ in ch}
            nz = {c_: -z[c_] for c_ in ch}
            lk = {c_: jnp.minimum(nz[c_], 0.0) - jnp.log2(1.0 + jnp.exp2(jnp.minimum(z[c_], nz[c_])))
                  for c_ in ch}
            if diagonal:
                lk = {c_: jnp.where(col < row, lk[c_], 0.0) for c_ in ch}
            hi = {c_: lk[c_].astype(BF16) for c_ in ch}
            lo = {c_: (lk[c_] - hi[c_].astype(F32)).astype(BF16) for c_ in ch}
            cs2 = {c_: jnp.dot(jnp.concatenate([hi[c_], lo[c_]], axis=1), u2, preferred_element_type=F32)
                   for c_ in ch}
            run = {}
            for h in range(nh):
                r = run_s[h]
                for j in range(len(k0s)):
                    run[(h, j)] = r
                    r = r + cs2[(h, j)][:, blk:]
                run_s[h] = r
            p = {c_: jnp.exp2(z[c_] + cs2[c_][:, :blk] + run[c_]) for c_ in ch}
            if diagonal:
                p = {c_: jnp.where(col < row, p[c_], 0.0) for c_ in ch}
            pv = {c_: jnp.dot(p[c_].astype(BF16), v_s[c_[0], pl.ds(k0s[c_[1]], blk), :],
                              preferred_element_type=F32) for c_ in ch}
            for h in range(nh):
                tot = pv[(h, 0)]
                for j in range(1, len(k0s)):
                    tot = tot + pv[(h, j)]
                acc_s[h] += tot

        blocks([q0], True)

        def pair(i, c2):
            ka = pl.multiple_of((qi - 1 - 2 * i) * blk, blk)
            kb = pl.multiple_of((qi - 2 - 2 * i) * blk, blk)
            blocks([ka, kb], False)
            return c2

        lax.fori_loop(0, jnp.right_shift(qi, 1), pair, 0)

        def last(i, c2):
            blocks([0], False)
            return c2

        lax.fori_loop(0, jnp.bitwise_and(qi, 1), last, 0)
        for h in range(nh):
            o_ref[0, pl.ds(q0, blk), h * dh:(h + 1) * dh] = acc_s[h].astype(o_ref.dtype)
        return c

    lax.fori_loop(0, s // blk, qloop, 0)


def _sb_call(xb, w, u2):
    bsz, s, d = xb.shape
    return pl.pallas_call(
        _sb_kernel,
        grid=(bsz,),
        in_specs=[pl.BlockSpec((1, s, d), lambda b: (b, 0, 0)),
                  pl.BlockSpec(w.shape, lambda b: (0, 0)),
                  pl.BlockSpec(u2.shape, lambda b: (0, 0))],
        out_specs=pl.BlockSpec((1, s, SB_HEADS * SB_DH), lambda b: (b, 0, 0)),
        out_shape=jax.ShapeDtypeStruct((bsz, s, SB_HEADS * SB_DH), BF16),
        scratch_shapes=[pltpu.VMEM((SB_HEADS, s, SB_DH), BF16)] * 3
        + [pltpu.VMEM((SB_HEADS, SB_BLOCK, SB_DH), F32), pltpu.VMEM((SB_HEADS, SB_BLOCK, SB_BLOCK), F32)],
        compiler_params=pltpu.CompilerParams(dimension_semantics=("arbitrary",),
                                             vmem_limit_bytes=VMEM_LIMIT),
        name="mixer_sb",
    )(xb, w, u2)


def _gla_kernel(xb_ref, w_ref, wlr_ref, wgu_ref, bg_ref, nw_ref, tri_ref, o_ref,
                q_s, k_s, v_s, g_s, gate_s, oi_s, qd_s, m_s, eb_s):
    s = xb_ref.shape[1]
    c = GLA_CHUNK
    dk = GLA_DK
    nck = s // c

    def proj(t, carry):
        r0 = pl.multiple_of(t * ROW_TILE, ROW_TILE)
        x = xb_ref[0, pl.ds(r0, ROW_TILE), :]
        p = jnp.dot(x, w_ref[0], preferred_element_type=F32)
        lr = jnp.dot(x, wlr_ref[...], preferred_element_type=F32)
        gpre = _bdot(lr, wgu_ref[0]) + bg_ref[0]
        g_s[pl.ds(r0, ROW_TILE), :] = jax.nn.log_sigmoid(gpre) * (1.0 / GLA_TAU)
        q_s[pl.ds(r0, ROW_TILE), :] = p[:, :dk] * dk ** -0.5
        k_s[pl.ds(r0, ROW_TILE), :] = p[:, dk:2 * dk]
        v_s[pl.ds(r0, ROW_TILE), :] = p[:, 2 * dk:3 * dk]
        gate_s[pl.ds(r0, ROW_TILE), :] = p[:, 3 * dk:]
        return carry

    lax.fori_loop(0, s // ROW_TILE, proj, 0)

    row = lax.broadcasted_iota(jnp.int32, (c, c), 0)
    col = lax.broadcasted_iota(jnp.int32, (c, c), 1)
    tri = tri_ref[...]

    def intra(grp, carry):
        u = range(CHUNK_UNROLL)
        ci = [grp * CHUNK_UNROLL + j for j in u]
        r0 = [pl.multiple_of(ci[j] * c, c) for j in u]
        q = [q_s[pl.ds(r0[j], c), :] for j in u]
        k = [k_s[pl.ds(r0[j], c), :] for j in u]
        v = [v_s[pl.ds(r0[j], c), :] for j in u]
        bc = [_split_dot_nt_left(tri, g_s[pl.ds(r0[j], c), :]) for j in u]
        blast = [bc[j][c - 1:c, :] for j in u]
        qd = [q[j] * jnp.exp(bc[j]) for j in u]
        ki = [k[j] * jnp.exp(-bc[j]) for j in u]
        kd = [k[j] * jnp.exp(blast[j] - bc[j]) for j in u]
        attn = [jnp.where(col <= row, _bdot_nt(qd[j], ki[j]), 0.0) for j in u]
        oi = [_bdot(attn[j], v[j]) for j in u]
        m = [_bdot_tn(v[j], kd[j]) for j in u]
        for j in u:
            oi_s[pl.ds(r0[j], c), :] = oi[j]
            qd_s[pl.ds(r0[j], c), :] = qd[j].astype(BF16)
            m_s[ci[j]] = m[j]
            eb_s[pl.ds(ci[j], 1), :] = jnp.exp(blast[j])
        return carry

    lax.fori_loop(0, nck // CHUNK_UNROLL, intra, 0)

    def inter(grp, st):
        for j in range(CHUNK_UNROLL):
            ci = grp * CHUNK_UNROLL + j
            r0 = pl.multiple_of(ci * c, c)
            o = oi_s[pl.ds(r0, c), :] + lax.dot_general(
                qd_s[pl.ds(r0, c), :], st.astype(BF16), (((1,), (1,)), ((), ())),
                preferred_element_type=F32)
            st = st * eb_s[pl.ds(ci, 1), :] + m_s[ci]
            ms = jnp.mean(o * o, axis=-1, keepdims=True)
            y = o * lax.rsqrt(ms + NORM_EPS) * nw_ref[...] * _silu(gate_s[pl.ds(r0, c), :])
            o_ref[0, pl.ds(r0, c), :] = y.astype(o_ref.dtype)
        return st

    lax.fori_loop(0, nck // CHUNK_UNROLL, inter, jnp.zeros((GLA_DV, dk), F32))


def _split_dot_nt_left(tri_bf16, x):
    hi = x.astype(BF16)
    lo = (x - hi.astype(F32)).astype(BF16)
    return (jnp.dot(tri_bf16, hi, preferred_element_type=F32)
            + jnp.dot(tri_bf16, lo, preferred_element_type=F32))


def _gla_call(xb, w, wlr, wgu, bg, nw, tri):
    bsz, s, d = xb.shape
    return pl.pallas_call(
        _gla_kernel,
        grid=(bsz, GLA_HEADS),
        in_specs=[pl.BlockSpec((1, s, d), lambda b, h: (b, 0, 0)),
                  pl.BlockSpec((1, d, 4 * GLA_DK), lambda b, h: (h, 0, 0)),
                  pl.BlockSpec(wlr.shape, lambda b, h: (0, 0)),
                  pl.BlockSpec((1, LANES, GLA_DK), lambda b, h: (h, 0, 0)),
                  pl.BlockSpec((1, 1, GLA_DK), lambda b, h: (h, 0, 0)),
                  pl.BlockSpec(nw.shape, lambda b, h: (0, 0)),
                  pl.BlockSpec(tri.shape, lambda b, h: (0, 0))],
        out_specs=pl.BlockSpec((1, s, GLA_DV), lambda b, h: (b, 0, h)),
        out_shape=jax.ShapeDtypeStruct((bsz, s, GLA_HEADS * GLA_DV), BF16),
        scratch_shapes=[pltpu.VMEM((s, GLA_DK), F32)] * 6
        + [pltpu.VMEM((s, GLA_DK), BF16), pltpu.VMEM((s // GLA_CHUNK, GLA_DV, GLA_DK), F32),
           pltpu.VMEM((s // GLA_CHUNK, GLA_DK), F32)],
        compiler_params=pltpu.CompilerParams(dimension_semantics=("arbitrary", "arbitrary"),
                                             vmem_limit_bytes=VMEM_LIMIT),
        name="mixer_gla",
    )(xb, w, wlr, wgu, bg, nw, tri)


def _gdn_kernel(xb_ref, w_ref, wbd_ref, cw_ref, hp_ref, nw_ref, tri_ref, o_ref,
                q_s, k_s, v_s, beta_s, g_s, gate_s, u_s, wq_s, kd_s, attn_s, egl_s, st_s):
    s = xb_ref.shape[1]
    c = GDN_CHUNK
    dk = GDN_DK
    dv = GDN_DV
    nck = s // c
    row8 = lax.broadcasted_iota(jnp.int32, (SUBLANES, 3 * dk), 0)
    row = lax.broadcasted_iota(jnp.int32, (c, c), 0)
    col = lax.broadcasted_iota(jnp.int32, (c, c), 1)
    eye = (row == col).astype(F32)
    tri = tri_ref[...]
    avg = jnp.full((c, dk), 1.0 / dk, BF16)

    def head(h, carry):
        def proj(t, tail):
            r0 = pl.multiple_of(t * ROW_TILE, ROW_TILE)
            x = xb_ref[0, pl.ds(r0, ROW_TILE), :]
            p = jnp.dot(x, w_ref[h], preferred_element_type=F32)
            bd = jnp.dot(x, wbd_ref[h], preferred_element_type=F32)
            u = p[:, :3 * dk]
            qkv = _silu(_causal_conv(u, tail, cw_ref.at[h], row8))
            q = qkv[:, :dk]
            k = qkv[:, dk:2 * dk]
            q = q * lax.rsqrt(jnp.sum(q * q, axis=-1, keepdims=True) + NORM_EPS) * dk ** -0.5
            k = k * lax.rsqrt(jnp.sum(k * k, axis=-1, keepdims=True) + NORM_EPS)
            q_s[pl.ds(r0, ROW_TILE), :] = q
            k_s[pl.ds(r0, ROW_TILE), :] = k
            v_s[pl.ds(r0, ROW_TILE), :] = qkv[:, 2 * dk:]
            gate_s[h, pl.ds(r0, ROW_TILE), :] = p[:, 3 * dk:]
            beta = jax.nn.sigmoid(bd[:, 0:1])
            g = -jnp.exp(hp_ref[h, 0:1, :]) * jax.nn.softplus(bd[:, 1:2] + hp_ref[h, 1:2, :])
            beta_s[pl.ds(r0, ROW_TILE), :] = jnp.broadcast_to(beta, (ROW_TILE, dk))
            g_s[pl.ds(r0, ROW_TILE), :] = g
            return u[ROW_TILE - SUBLANES:, :]

        lax.fori_loop(0, s // ROW_TILE, proj, jnp.zeros((SUBLANES, 3 * dk), F32))

        def intra(grp, c1):
            js = range(CHUNK_UNROLL)
            ci = [grp * CHUNK_UNROLL + j for j in js]
            r0 = [pl.multiple_of(ci[j] * c, c) for j in js]
            q = [q_s[pl.ds(r0[j], c), :] for j in js]
            k = [k_s[pl.ds(r0[j], c), :] for j in js]
            v = [v_s[pl.ds(r0[j], c), :] for j in js]
            beta = [beta_s[pl.ds(r0[j], c), :] for j in js]
            gc = [_split_dot_nt_left(tri, g_s[pl.ds(r0[j], c), :]) for j in js]
            gc_row = [_split_dot_nt(avg, gc[j]) for j in js]
            dec = [jnp.exp(jnp.minimum(gc[j][:, :c] - gc_row[j], 0.0)) for j in js]
            kb = [k[j] * beta[j] for j in js]
            qk = [_bdot_nt(jnp.concatenate([kb[j], q[j]], axis=0), k[j]) for j in js]
            a = [jnp.where(col < row, qk[j][:c] * dec[j], 0.0) for j in js]
            attn = [jnp.where(col <= row, qk[j][c:] * dec[j], 0.0) for j in js]
            tinv = [eye - a[j] for j in js]
            ak = a
            n = 1
            while n < c // 2:
                ak = [_bdot(ak[j], ak[j]) for j in js]
                tinv = [tinv[j] + _bdot(tinv[j], ak[j]) for j in js]
                n *= 2
            egc = [jnp.exp(gc[j]) for j in js]
            uw = [_bdot(tinv[j], jnp.concatenate([v[j] * beta[j], kb[j] * egc[j]], axis=1))
                  for j in js]
            for j in js:
                gl = gc[j][c - 1:c, :]
                u_s[h, pl.ds(r0[j], c), :] = uw[j][:, :dv]
                wq_s[h, ci[j]] = jnp.concatenate([uw[j][:, dv:], q[j] * egc[j]], axis=0).astype(BF16)
                kd_s[h, pl.ds(r0[j], c), :] = (k[j] * jnp.exp(gl - gc[j])).astype(BF16)
                attn_s[h, pl.ds(r0[j], c), :] = attn[j].astype(BF16)
                egl_s[h, pl.ds(ci[j], 1), :] = jnp.exp(gl)
            return c1

        lax.fori_loop(0, nck // CHUNK_UNROLL, intra, 0)
        return carry

    lax.fori_loop(0, GDN_HEADS, head, 0)

    st_s[...] = jnp.zeros_like(st_s)

    def inter(ci, c2):
        r0 = pl.multiple_of(ci * c, c)
        hs = range(GDN_HEADS)
        tn = (((0,), (0,)), ((), ()))
        st = [st_s[h] for h in hs]
        ws_qs = [jnp.dot(wq_s[h, ci], st[h].astype(BF16), preferred_element_type=F32) for h in hs]
        v_new = [(u_s[h, pl.ds(r0, c), :] - ws_qs[h][:c]).astype(BF16) for h in hs]
        o = [ws_qs[h][c:] + jnp.dot(attn_s[h, pl.ds(r0, c), :], v_new[h], preferred_element_type=F32)
             for h in hs]
        kv = [lax.dot_general(kd_s[h, pl.ds(r0, c), :], v_new[h], tn, preferred_element_type=F32)
              for h in hs]
        for h in hs:
            st_s[h] = st[h] * egl_s[h, pl.ds(ci, 1), :] + kv[h]
            ms = jnp.mean(o[h] * o[h], axis=-1, keepdims=True)
            y = o[h] * lax.rsqrt(ms + NORM_EPS) * nw_ref[...] * _silu(gate_s[h, pl.ds(r0, c), :])
            o_ref[0, pl.ds(r0, c), h * dv:(h + 1) * dv] = y.astype(o_ref.dtype)
        return c2

    lax.fori_loop(0, nck, inter, 0)


def _gdn_call(xb, w, wbd, cw, hp, nw, tri):
    bsz, s, d = xb.shape
    nh = GDN_HEADS
    nck = s // GDN_CHUNK
    const = lambda a: pl.BlockSpec(a.shape, lambda b: (0,) * a.ndim)
    return pl.pallas_call(
        _gdn_kernel,
        grid=(bsz,),
        in_specs=[pl.BlockSpec((1, s, d), lambda b: (b, 0, 0)),
                  const(w), const(wbd), const(cw), const(hp), const(nw), const(tri)],
        out_specs=pl.BlockSpec((1, s, nh * GDN_DV), lambda b: (b, 0, 0)),
        out_shape=jax.ShapeDtypeStruct((bsz, s, nh * GDN_DV), BF16),
        scratch_shapes=[pltpu.VMEM((s, GDN_DK), F32)] * 5
        + [pltpu.VMEM((nh, s, GDN_DV), F32),
           pltpu.VMEM((nh, s, GDN_DV), F32),
           pltpu.VMEM((nh, nck, 2 * GDN_CHUNK, GDN_DK), BF16),
           pltpu.VMEM((nh, s, GDN_DK), BF16),
           pltpu.VMEM((nh, s, GDN_CHUNK), BF16),
           pltpu.VMEM((nh, nck, GDN_DK), F32),
           pltpu.VMEM((nh, GDN_DK, GDN_DV), F32)],
        compiler_params=pltpu.CompilerParams(dimension_semantics=("arbitrary",),
                                             vmem_limit_bytes=VMEM_LIMIT),
        name="mixer_gdn",
    )(xb, w, wbd, cw, hp, nw, tri)


def _merge_kernel(x_ref, xb_ref, ya_ref, yb_ref, yc_ref, yd_ref, wm_ref, wb_ref, wo_ref,
                  g_ref, b_ref, o_ref, ob_ref):
    xb = xb_ref[...]
    merged = None
    for n, y_ref in enumerate((ya_ref, yb_ref, yc_ref, yd_ref)):
        gate = jax.nn.sigmoid(jnp.dot(xb, wm_ref[:, n * D_MODEL:(n + 1) * D_MODEL],
                                      preferred_element_type=F32))
        term = gate * jnp.dot(y_ref[...], wb_ref[n], preferred_element_type=F32)
        merged = term if merged is None else merged + term
    h = _bdot(merged, wo_ref[...])
    y = _layer_norm(DN_ALPHA * x_ref[...] + h, g_ref[...], b_ref[...])
    o_ref[...] = y
    ob_ref[...] = y.astype(BF16)


def _merge_call(x, xb, ya, yb, yc, yd, wm, wb, wo, g, b):
    t, d = x.shape
    tm = 256
    tok = lambda width: pl.BlockSpec((tm, width), lambda i: (i, 0))
    const = lambda shape: pl.BlockSpec(shape, lambda i: (0,) * len(shape),
                                       pipeline_mode=pl.Buffered(1))
    return pl.pallas_call(
        _merge_kernel,
        grid=(t // tm,),
        in_specs=[tok(d), tok(d), tok(BRANCH_WIDTH), tok(BRANCH_WIDTH), tok(BRANCH_WIDTH),
                  tok(BRANCH_WIDTH), const(wm.shape), const(wb.shape), const(wo.shape),
                  const(g.shape), const(b.shape)],
        out_specs=[tok(d), tok(d)],
        out_shape=[jax.ShapeDtypeStruct((t, d), F32), jax.ShapeDtypeStruct((t, d), BF16)],
        compiler_params=pltpu.CompilerParams(dimension_semantics=("arbitrary",),
                                             vmem_limit_bytes=VMEM_LIMIT),
        name="merge",
    )(x, xb, ya, yb, yc, yd, wm, wb, wo, g, b)


def _route(scores, biased):
    s = [scores[e:e + 1, :] for e in range(N_EXPERTS)]
    b = [biased[e:e + 1, :] for e in range(N_EXPERTS)]
    gs = []
    for g in range(N_GROUPS):
        m = b[EXPERTS_PER_GROUP * g:EXPERTS_PER_GROUP * (g + 1)]
        best = None
        for i in range(EXPERTS_PER_GROUP):
            for j in range(i + 1, EXPERTS_PER_GROUP):
                ps = m[i] + m[j]
                best = ps if best is None else jnp.maximum(best, ps)
        gs.append(best)
    gidx = jnp.zeros_like(gs[0], dtype=jnp.int32)
    gval = gs[0]
    for g in range(1, N_GROUPS):
        take = gs[g] > gval
        gidx = jnp.where(take, g, gidx)
        gval = jnp.where(take, gs[g], gval)
    w = []
    for e in range(N_EXPERTS):
        g = e // EXPERTS_PER_GROUP
        beaten = jnp.zeros_like(gidx)
        for j in range(EXPERTS_PER_GROUP * g, EXPERTS_PER_GROUP * (g + 1)):
            if j == e:
                continue
            wins = (b[j] >= b[e]) if j < e else (b[j] > b[e])
            beaten = beaten + wins.astype(jnp.int32)
        sel = (gidx == g) & (beaten < 2)
        w.append(jnp.where(sel, s[e], 0.0))
    tot = w[0]
    for e in range(1, N_EXPERTS):
        tot = tot + w[e]
    return [w[e] / tot for e in range(N_EXPERTS)], gidx


def _moe_kernel(x_ref, xb_ref, wr_ref, rb_ref, su_ref, wgu_ref, wd_ref, g_ref, b_ref,
                o_ref, ob_ref, acc_ref, tr_ref, slotr_ref, slotc_ref, chl_ref, cnt_ref):
    g = pl.program_id(1)
    tm = x_ref.shape[0]
    cc = MOE_CHUNK

    @pl.when(g == 0)
    def _():
        logits = lax.dot_general(wr_ref[...], x_ref[...], (((1,), (1,)), ((), ())),
                                 precision=lax.Precision.HIGHEST, preferred_element_type=F32)
        scores = jax.nn.sigmoid(logits)
        rows, gidx = _route(scores, scores + rb_ref[...])
        member = [(gidx == k).astype(F32) for k in range(N_GROUPS)]
        mm = jnp.concatenate(member + [jnp.zeros((SUBLANES - N_GROUPS, tm), F32)], axis=0)
        rank = jnp.dot(mm.astype(BF16), su_ref[...], preferred_element_type=F32)
        slot = jnp.where(mm > 0.0, rank, -1.0)
        slotr_ref[...] = slot
        for k in range(N_GROUPS):
            cnt_ref[k] = jnp.sum(member[k]).astype(jnp.int32)
        tr_ref[...] = jnp.zeros_like(tr_ref)
        for k in range(N_EXPERTS):
            tr_ref[k:k + 1, :] = rows[k]
        tr_ref[N_EXPERTS:N_EXPERTS + SUBLANES, :] = slot
        tc = tr_ref[...].T
        hi = tc.astype(BF16)
        chl_ref[:, :LANES] = hi
        chl_ref[:, LANES:] = (tc - hi.astype(F32)).astype(BF16)
        for k in range(N_GROUPS):
            slotc_ref[k] = jnp.broadcast_to(tc[:, N_EXPERTS + k:N_EXPERTS + k + 1], (tm, LANES))
        acc_ref[...] = jnp.zeros_like(acc_ref)

    slot_row = slotr_ref[pl.ds(g, 1), :]
    slot_col = slotc_ref[g]
    slot_col = jnp.concatenate([slot_col] * (cc // LANES), axis=1)
    lane = lax.broadcasted_iota(jnp.int32, (cc, LANES), 1)
    row_i = lax.broadcasted_iota(jnp.int32, (cc, tm), 0).astype(F32)
    col_i = lax.broadcasted_iota(jnp.int32, (tm, cc), 1).astype(F32)

    def chunk(ci, carry):
        base = (ci * cc).astype(F32)
        gather = (slot_row == row_i + base).astype(BF16)
        xg = jnp.dot(gather, xb_ref[...], preferred_element_type=F32).astype(BF16)
        cw2 = jnp.dot(gather, chl_ref[...], preferred_element_type=F32)
        cw = cw2[:, :LANES] + cw2[:, LANES:]
        yg = None
        for k in range(EXPERTS_PER_GROUP):
            hgu = jnp.dot(xg, wgu_ref[0, k], preferred_element_type=F32)
            h = _silu(hgu[:, :D_EXPERT]) * hgu[:, D_EXPERT:]
            ce = jnp.sum(jnp.where(lane == g * EXPERTS_PER_GROUP + k, cw, 0.0), axis=-1, keepdims=True)
            term = ce * _bdot(h, wd_ref[0, k])
            yg = term if yg is None else yg + term
        scatter = (slot_col == col_i + base).astype(BF16)
        acc_ref[...] += jnp.dot(scatter, yg.astype(BF16), preferred_element_type=F32)
        return carry

    lax.fori_loop(0, (cnt_ref[g] + cc - 1) // cc, chunk, 0)

    @pl.when(g == N_GROUPS - 1)
    def _():
        out = _layer_norm(DN_ALPHA * x_ref[...] + acc_ref[...], g_ref[...], b_ref[...])
        o_ref[...] = out
        ob_ref[...] = out.astype(BF16)


def _moe_call(x, xb, wr, rb, su, wgu, wd, g, b):
    t, d = x.shape
    tm = MOE_TILE
    epg = EXPERTS_PER_GROUP
    tok = lambda mode=None: pl.BlockSpec((tm, d), lambda i, k: (i, 0), pipeline_mode=mode)
    const = lambda shape, mode=None: pl.BlockSpec(shape, lambda i, k: (0,) * len(shape),
                                                  pipeline_mode=mode)
    return pl.pallas_call(
        _moe_kernel,
        grid=(t // tm, N_GROUPS),
        in_specs=[tok(pl.Buffered(1)), tok(pl.Buffered(1)), const(wr.shape), const(rb.shape),
                  const(su.shape, pl.Buffered(1)),
                  pl.BlockSpec((1, epg, d, 2 * D_EXPERT), lambda i, k: (k, 0, 0, 0)),
                  pl.BlockSpec((1, epg, D_EXPERT, d), lambda i, k: (k, 0, 0, 0)),
                  const(g.shape), const(b.shape)],
        out_specs=[tok(), tok()],
        out_shape=[jax.ShapeDtypeStruct((t, d), F32), jax.ShapeDtypeStruct((t, d), BF16)],
        scratch_shapes=[pltpu.VMEM((tm, d), F32),
                        pltpu.VMEM((LANES, tm), F32),
                        pltpu.VMEM((SUBLANES, tm), F32),
                        pltpu.VMEM((N_GROUPS, tm, LANES), F32),
                        pltpu.VMEM((tm, 2 * LANES), BF16),
                        pltpu.SMEM((N_GROUPS,), jnp.int32)],
        compiler_params=pltpu.CompilerParams(dimension_semantics=("arbitrary", "arbitrary"),
                                             vmem_limit_bytes=VMEM_LIMIT),
        name="moe",
    )(x, xb, wr, rb, su, wgu, wd, g, b)


def _block_diag(w):
    g, n, _ = w.shape
    eye = jnp.eye(g, dtype=w.dtype)
    return (eye[:, None, :, None] * w[:, :, None, :]).reshape(g * n, g * n)


def _per_head(w, off, n_heads, width, parts):
    cols = [w[:, off + p * n_heads * width: off + (p + 1) * n_heads * width]
            .reshape(w.shape[0], n_heads, width) for p in parts]
    return jnp.transpose(jnp.concatenate(cols, axis=2), (1, 0, 2))


def _row(v):
    return v.reshape(1, -1).astype(F32)


def kernel(x, w_in, conv_a_w, conv_a_b, rg_w_a, rg_b_a, rg_w_x, rg_b_x, rg_lambda, gdn_conv_w, gdn_a_log, gdn_dt_bias, gdn_norm_w, gla_w_gate_up, gla_b_gate, gla_norm_w, w_branch, w_out, ln1_g, ln1_b, w_router, router_bias, w_gate, w_up, w_down, ln2_g, ln2_b):
    bsz, s, d = x.shape
    t = bsz * s
    c = GDN_CHUNK
    ii = jnp.arange(SB_BLOCK)
    u2 = jnp.concatenate([(ii[:, None] >= ii[None, :]).astype(BF16),
                          jnp.ones((SB_BLOCK, SB_BLOCK), BF16)], axis=1)
    u2 = jnp.concatenate([u2, u2], axis=0)
    jj = jnp.arange(c)
    tri = (jj[:, None] >= jj[None, :]).astype(BF16)
    wr = w_router.astype(F32).T
    rb = router_bias.astype(F32).reshape(N_EXPERTS, 1)
    kk = jnp.arange(MOE_TILE)
    su = (kk[:, None] < kk[None, :]).astype(BF16)

    xf = x.reshape(t, d)
    xb = xf.astype(BF16)
    for l in range(DEPTH):
        wl = w_in[l]
        wlb = wl.astype(BF16)
        xb3 = xb.reshape(bsz, s, d)

        wg = jnp.concatenate([_block_diag(rg_w_a[l]), _block_diag(rg_w_x[l])], axis=1).astype(BF16)
        bg = jnp.concatenate([rg_b_a[l], rg_b_x[l]]).reshape(1, -1).astype(F32)
        ya = _lru_call(xb3, wlb[:, _OFF_A:_OFF_A + LRU_WIDTH], conv_a_w[l].astype(F32),
                       _row(conv_a_b[l]), wg, bg, _row(rg_lambda[l]))

        w_gdn = jnp.concatenate([_per_head(wlb, _OFF_GDN_QKV, GDN_HEADS, GDN_DK, (0, 1, 2)),
                                 _per_head(wlb, _OFF_GDN_GATE, GDN_HEADS, GDN_DV, (0,))], axis=2)
        wbd = jnp.stack([wlb[:, _OFF_GDN_BETA:_OFF_GDN_BETA + GDN_HEADS],
                         wlb[:, _OFF_GDN_DECAY:_OFF_GDN_DECAY + GDN_HEADS]], axis=2)
        wbd = jnp.pad(jnp.transpose(wbd, (1, 0, 2)), ((0, 0), (0, 0), (0, LANES - 2)))
        cw_gdn = jnp.transpose(gdn_conv_w[l].astype(F32).reshape(CONV_WIDTH, 3, GDN_HEADS, GDN_DK),
                               (2, 0, 1, 3)).reshape(GDN_HEADS, CONV_WIDTH, 3 * GDN_DK)
        hp = jnp.zeros((GDN_HEADS, SUBLANES, GDN_DK), F32)
        hp = hp.at[:, 0, :].set(gdn_a_log[l].astype(F32)[:, None])
        hp = hp.at[:, 1, :].set(gdn_dt_bias[l].astype(F32)[:, None])
        yb = _gdn_call(xb3, w_gdn, wbd, cw_gdn, hp, _row(gdn_norm_w[l]), tri)

        yc = _sb_call(xb3, wlb[:, _OFF_SB:_OFF_SB + 3 * SB_HEADS * SB_DH], u2)

        q_off = _OFF_GLA_QKV
        w_gla = jnp.concatenate([
            _per_head(wlb, q_off, GLA_HEADS, GLA_DK, (0, 1)),
            _per_head(wlb, q_off + 2 * GLA_HEADS * GLA_DK, GLA_HEADS, GLA_DV, (0,)),
            _per_head(wlb, _OFF_GLA_GATE, GLA_HEADS, GLA_DV, (0,))], axis=2)
        wlr = jnp.pad(wlb[:, _OFF_GLA_LR:_OFF_GLA_LR + GLA_GATE_RANK],
                      ((0, 0), (0, LANES - GLA_GATE_RANK)))
        wgu = jnp.pad(jnp.transpose(gla_w_gate_up[l].reshape(GLA_GATE_RANK, GLA_HEADS, GLA_DK), (1, 0, 2)),
                      ((0, 0), (0, LANES - GLA_GATE_RANK), (0, 0))).astype(BF16)
        bgl = gla_b_gate[l].astype(F32).reshape(GLA_HEADS, 1, GLA_DK)
        yd = _gla_call(xb3, w_gla, wlr, wgu, bgl, _row(gla_norm_w[l]), tri)

        xf, xb = _merge_call(xf, xb, ya.reshape(t, -1), yb.reshape(t, -1), yc.reshape(t, -1),
                             yd.reshape(t, -1), wlb[:, _OFF_MERGE:], w_branch[l].astype(BF16),
                             w_out[l].astype(BF16), _row(ln1_g[l]), _row(ln1_b[l]))

        w_gu = jnp.concatenate([w_gate[l].astype(BF16), w_up[l].astype(BF16)], axis=2)
        w_gu = w_gu.reshape(N_GROUPS, EXPERTS_PER_GROUP, d, 2 * D_EXPERT)
        w_dn = w_down[l].astype(BF16).reshape(N_GROUPS, EXPERTS_PER_GROUP, D_EXPERT, d)
        xf, xb = _moe_call(xf, xb, wr, rb, su, w_gu, w_dn, _row(ln2_g[l]), _row(ln2_b[l]))
    return xf.reshape(bsz, s, d)
```

```python
import functools

import jax
import jax.numpy as jnp
from jax import lax
from jax.experimental import pallas as pl
from jax.experimental.pallas import tpu as pltpu

F32 = jnp.float32
BF16 = jnp.bfloat16

D_MODEL = 1024
DEPTH = 4
LRU_WIDTH = 512
LRU_BLOCKS = 8
LRU_C = 8.0
CONV_WIDTH = 4
GDN_HEADS = 4
GDN_DK = 128
GDN_DV = 128
GDN_CHUNK = 64
SB_HEADS = 8
SB_DH = 64
SB_BLOCK = 128
GLA_HEADS = 4
GLA_DK = 128
GLA_DV = 128
GLA_GATE_RANK = 16
GLA_TAU = 16.0
GLA_CHUNK = 64
N_BRANCHES = 4
BRANCH_WIDTH = 512
N_EXPERTS = 16
N_GROUPS = 4
EXPERTS_PER_GROUP = 4
D_EXPERT = 512
LN_EPS = 1e-5
NORM_EPS = 1e-6
DN_ALPHA = (2 * DEPTH) ** 0.25
LOG2E = 1.4426950408889634

_OFF_A = 0
_OFF_GDN_QKV = _OFF_A + LRU_WIDTH
_OFF_GDN_BETA = _OFF_GDN_QKV + 3 * GDN_HEADS * GDN_DK
_OFF_GDN_DECAY = _OFF_GDN_BETA + GDN_HEADS
_OFF_GDN_GATE = _OFF_GDN_DECAY + GDN_HEADS
_OFF_SB = _OFF_GDN_GATE + GDN_HEADS * GDN_DV
_OFF_GLA_QKV = _OFF_SB + 3 * SB_HEADS * SB_DH
_OFF_GLA_LR = _OFF_GLA_QKV + GLA_HEADS * (2 * GLA_DK + GLA_DV)
_OFF_GLA_GATE = _OFF_GLA_LR + GLA_GATE_RANK
_OFF_MERGE = _OFF_GLA_GATE + GLA_HEADS * GLA_DV

LANES = 128
SUBLANES = 8
ROW_TILE = 256
CHUNK_UNROLL = 8
GDN_UNROLL = 16
MOE_TILE = 1024
MOE_CHUNK_LOG2 = 8
MOE_CHUNK = 1 << MOE_CHUNK_LOG2
VMEM_LIMIT = 56 * 1024 * 1024


def _bdot(a, b):
    return jnp.dot(a.astype(BF16), b.astype(BF16), preferred_element_type=F32)


def _bdot_nt(a, b):
    return lax.dot_general(a.astype(BF16), b.astype(BF16), (((1,), (1,)), ((), ())),
                           preferred_element_type=F32)


def _bdot_tn(a, b):
    return lax.dot_general(a.astype(BF16), b.astype(BF16), (((0,), (0,)), ((), ())),
                           preferred_element_type=F32)


def _split_dot(a, b_bf16):
    hi = a.astype(BF16)
    lo = (a - hi.astype(F32)).astype(BF16)
    return (jnp.dot(hi, b_bf16, preferred_element_type=F32)
            + jnp.dot(lo, b_bf16, preferred_element_type=F32))


def _split_dot_nt(a_bf16, b):
    hi = b.astype(BF16)
    lo = (b - hi.astype(F32)).astype(BF16)
    dn = (((1,), (1,)), ((), ()))
    return (lax.dot_general(a_bf16, hi, dn, preferred_element_type=F32)
            + lax.dot_general(a_bf16, lo, dn, preferred_element_type=F32))


def _silu(x):
    return x * jax.nn.sigmoid(x)


def _layer_norm(x, g, b):
    mu = jnp.mean(x, axis=-1, keepdims=True)
    xc = x - mu
    var = jnp.mean(xc * xc, axis=-1, keepdims=True)
    return xc * lax.rsqrt(var + LN_EPS) * g + b


def _shift_rows(u, tail, s, row8):
    ur = pltpu.roll(u, s, 0)
    top = jnp.where(row8 >= s, ur[:SUBLANES], pltpu.roll(tail, s, 0))
    return jnp.concatenate([top, ur[SUBLANES:]], axis=0)


def _causal_conv(u, tail, cw_ref, row8):
    y = u * cw_ref[CONV_WIDTH - 1:CONV_WIDTH, :]
    for s in range(1, CONV_WIDTH):
        k = CONV_WIDTH - 1 - s
        y = y + _shift_rows(u, tail, s, row8) * cw_ref[k:k + 1, :]
    return y


def _lru_kernel(xb_ref, w_ref, cw_ref, cb_ref, wg_ref, bg_ref, lam_ref, o_ref, tail_ref, h_ref):
    t = pl.program_id(1)
    tt = xb_ref.shape[1]
    w = LRU_WIDTH

    @pl.when(t == 0)
    def _():
        tail_ref[...] = jnp.zeros_like(tail_ref)
        h_ref[...] = jnp.zeros_like(h_ref)

    u = jnp.dot(xb_ref[0], w_ref[...], preferred_element_type=F32)
    row = lax.broadcasted_iota(jnp.int32, (tt, w), 0)
    row8 = lax.broadcasted_iota(jnp.int32, (SUBLANES, w), 0)
    xc = _causal_conv(u, tail_ref[...], cw_ref, row8) + cb_ref[...]
    tail_ref[...] = u[tt - SUBLANES:, :]

    gates = _bdot(xc, wg_ref[...]) + bg_ref[...]
    r = jax.nn.sigmoid(gates[:, :w])
    i = jax.nn.sigmoid(gates[:, w:])
    log_a = (-LRU_C) * r * jax.nn.softplus(-lam_ref[...])
    a = jnp.exp(log_a)
    mult = jnp.sqrt(-jnp.tanh(log_a) * (a * a + 1.0))
    mult = jnp.where(row + t * tt == 0, 1.0, mult)
    b = mult * i * xc

    d = 1
    while d < tt:
        keep = row >= d
        a_sh = jnp.where(keep, pltpu.roll(a, d, 0), 1.0)
        b_sh = jnp.where(keep, pltpu.roll(b, d, 0), 0.0)
        b = a * b_sh + b
        a = a * a_sh
        d *= 2
    h = a * h_ref[0:1, :] + b
    h_ref[...] = jnp.broadcast_to(h[tt - 1:tt, :], h_ref.shape)
    o_ref[0] = h.astype(o_ref.dtype)


def _lru_call(xb, w, cw, cb, wg, bg, lam):
    bsz, s, d = xb.shape
    tt = ROW_TILE
    const = lambda shape: pl.BlockSpec(shape, lambda b, t: (0,) * len(shape))
    return pl.pallas_call(
        _lru_kernel,
        grid=(bsz, s // tt),
        in_specs=[pl.BlockSpec((1, tt, d), lambda b, t: (b, t, 0)),
                  const(w.shape), const(cw.shape), const(cb.shape), const(wg.shape),
                  const(bg.shape), const(lam.shape)],
        out_specs=pl.BlockSpec((1, tt, LRU_WIDTH), lambda b, t: (b, t, 0)),
        out_shape=jax.ShapeDtypeStruct((bsz, s, LRU_WIDTH), BF16),
        scratch_shapes=[pltpu.VMEM((SUBLANES, LRU_WIDTH), F32), pltpu.VMEM((SUBLANES, LRU_WIDTH), F32)],
        compiler_params=pltpu.CompilerParams(dimension_semantics=("arbitrary", "arbitrary"),
                                             vmem_limit_bytes=VMEM_LIMIT),
        name="mixer_lru",
    )(xb, w, cw, cb, wg, bg, lam)


def _sb_kernel(xb_ref, w_ref, u2_ref, o_ref, q_s, k_s, v_s, acc_s, run_s):
    s = xb_ref.shape[1]
    blk = SB_BLOCK
    dh = SB_DH
    nh = SB_HEADS
    hd = nh * dh

    def proj(t, c):
        r0 = pl.multiple_of(t * ROW_TILE, ROW_TILE)
        qkv = jnp.dot(xb_ref[0, pl.ds(r0, ROW_TILE), :], w_ref[...], preferred_element_type=F32)
        for h in range(nh):
            q_s[h, pl.ds(r0, ROW_TILE), :] = (qkv[:, h * dh:(h + 1) * dh] * (dh ** -0.5 * LOG2E)).astype(BF16)
            k_s[h, pl.ds(r0, ROW_TILE), :] = qkv[:, hd + h * dh:hd + (h + 1) * dh].astype(BF16)
            v_s[h, pl.ds(r0, ROW_TILE), :] = qkv[:, 2 * hd + h * dh:2 * hd + (h + 1) * dh].astype(BF16)
        return c

    lax.fori_loop(0, s // ROW_TILE, proj, 0)

    row = lax.broadcasted_iota(jnp.int32, (blk, blk), 0)
    col = lax.broadcasted_iota(jnp.int32, (blk, blk), 1)
    u2 = u2_ref[...]

    def qloop(qi, c):
        q0 = pl.multiple_of(qi * blk, blk)
        acc_s[...] = jnp.zeros_like(acc_s)
        run_s[...] = jnp.zeros_like(run_s)

        def blocks(k0s, diagonal):
            ch = [(h, j) for j in range(len(k0s)) for h in range(nh)]
            nt = (((1,), (1,)), ((), ()))
            z = {c_: lax.dot_general(q_s[c_[0], pl.ds(q0, blk), :], k_s[c_[0], pl.ds(k0s[c_[1]], blk), :],
                                     nt, preferred_element_type=F32) for c_ in ch}
            nz = {c_: -z[c_] for c_ in ch}
            lk = {c_: jnp.minimum(nz[c_], 0.0) - jnp.log2(1.0 + jnp.exp2(jnp.minimum(z[c_], nz[c_])))
                  for c_ in ch}
            lk = {c_: jnp.where(col < row, lk[c_], 0.0) if diagonal[c_[1]] else lk[c_] for c_ in ch}
            hi = {c_: lk[c_].astype(BF16) for c_ in ch}
            lo = {c_: (lk[c_] - hi[c_].astype(F32)).astype(BF16) for c_ in ch}
            cs2 = {c_: jnp.dot(jnp.concatenate([hi[c_], lo[c_]], axis=1), u2, preferred_element_type=F32)
                   for c_ in ch}
            run = {}
            for h in range(nh):
                r = run_s[h]
                for j in range(len(k0s)):
                    run[(h, j)] = r
                    r = r + cs2[(h, j)][:, blk:]
                run_s[h] = r
            p = {c_: jnp.exp2(z[c_] + cs2[c_][:, :blk] + run[c_]) for c_ in ch}
            p = {c_: jnp.where(col < row, p[c_], 0.0) if diagonal[c_[1]] else p[c_] for c_ in ch}
            pv = {c_: jnp.dot(p[c_].astype(BF16), v_s[c_[0], pl.ds(k0s[c_[1]], blk), :],
                              preferred_element_type=F32) for c_ in ch}
            for h in range(nh):
                tot = pv[(h, 0)]
                for j in range(1, len(k0s)):
                    tot = tot + pv[(h, j)]
                acc_s[h] += tot

        blocks([q0], (True,))

        def pair(i, c2):
            ka = pl.multiple_of((qi - 1 - 2 * i) * blk, blk)
            kb = pl.multiple_of((qi - 2 - 2 * i) * blk, blk)
            blocks([ka, kb], (False, False))
            return c2

        lax.fori_loop(0, jnp.right_shift(qi, 1), pair, 0)

        def last(i, c2):
            blocks([0], (False,))
            return c2

        lax.fori_loop(0, jnp.bitwise_and(qi, 1), last, 0)
        for h in range(nh):
            o_ref[0, pl.ds(q0, blk), h * dh:(h + 1) * dh] = acc_s[h].astype(o_ref.dtype)
        return c

    lax.fori_loop(0, s // blk, qloop, 0)


def _sb_call(xb, w, u2):
    bsz, s, d = xb.shape
    return pl.pallas_call(
        _sb_kernel,
        grid=(bsz,),
        in_specs=[pl.BlockSpec((1, s, d), lambda b: (b, 0, 0)),
                  pl.BlockSpec(w.shape, lambda b: (0, 0)),
                  pl.BlockSpec(u2.shape, lambda b: (0, 0))],
        out_specs=pl.BlockSpec((1, s, SB_HEADS * SB_DH), lambda b: (b, 0, 0)),
        out_shape=jax.ShapeDtypeStruct((bsz, s, SB_HEADS * SB_DH), BF16),
        scratch_shapes=[pltpu.VMEM((SB_HEADS, s, SB_DH), BF16)] * 3
        + [pltpu.VMEM((SB_HEADS, SB_BLOCK, SB_DH), F32), pltpu.VMEM((SB_HEADS, SB_BLOCK, SB_BLOCK), F32)],
        compiler_params=pltpu.CompilerParams(dimension_semantics=("arbitrary",),
                                             vmem_limit_bytes=VMEM_LIMIT),
        name="mixer_sb",
    )(xb, w, u2)


def _gla_kernel(xb_ref, w_ref, wlr_ref, wgu_ref, bg_ref, nw_ref, tri_ref, o_ref,
                q_s, k_s, v_s, g_s, gate_s, oi_s, qd_s, m_s, eb_s):
    s = xb_ref.shape[1]
    c = GLA_CHUNK
    dk = GLA_DK
    nck = s // c

    def proj(t, carry):
        us = range(2)
        r0 = [pl.multiple_of((2 * t + i) * ROW_TILE, ROW_TILE) for i in us]
        x = [xb_ref[0, pl.ds(r0[i], ROW_TILE), :] for i in us]
        lr = [jnp.dot(x[i], wlr_ref[...], preferred_element_type=F32) for i in us]
        p = [jnp.dot(x[i], w_ref[0], preferred_element_type=F32) for i in us]
        gpre = [_bdot(lr[i], wgu_ref[0]) + bg_ref[0] for i in us]
        for i in us:
            rows = pl.ds(r0[i], ROW_TILE)
            g_s[rows, :] = jax.nn.log_sigmoid(gpre[i]) * (1.0 / GLA_TAU)
            q_s[rows, :] = p[i][:, :dk] * dk ** -0.5
            k_s[rows, :] = p[i][:, dk:2 * dk]
            v_s[rows, :] = p[i][:, 2 * dk:3 * dk]
            gate_s[rows, :] = p[i][:, 3 * dk:]
        return carry

    lax.fori_loop(0, s // (2 * ROW_TILE), proj, 0)

    row = lax.broadcasted_iota(jnp.int32, (c, c), 0)
    col = lax.broadcasted_iota(jnp.int32, (c, c), 1)
    tri = tri_ref[...]

    def intra(grp, carry):
        u = range(CHUNK_UNROLL)
        ci = [grp * CHUNK_UNROLL + j for j in u]
        r0 = [pl.multiple_of(ci[j] * c, c) for j in u]
        q = [q_s[pl.ds(r0[j], c), :] for j in u]
        k = [k_s[pl.ds(r0[j], c), :] for j in u]
        v = [v_s[pl.ds(r0[j], c), :] for j in u]
        bc = [_split_dot_nt_left(tri, g_s[pl.ds(r0[j], c), :]) for j in u]
        blast = [bc[j][c - 1:c, :] for j in u]
        qd = [q[j] * jnp.exp(bc[j]) for j in u]
        ki = [k[j] * jnp.exp(-bc[j]) for j in u]
        kd = [k[j] * jnp.exp(blast[j] - bc[j]) for j in u]
        attn = [jnp.where(col <= row, _bdot_nt(qd[j], ki[j]), 0.0) for j in u]
        oi = [_bdot(attn[j], v[j]) for j in u]
        m = [_bdot_tn(v[j], kd[j]) for j in u]
        for j in u:
            oi_s[pl.ds(r0[j], c), :] = oi[j]
            qd_s[pl.ds(r0[j], c), :] = qd[j].astype(BF16)
            m_s[ci[j]] = m[j]
            eb_s[pl.ds(ci[j], 1), :] = jnp.exp(blast[j])
        return carry

    lax.fori_loop(0, nck // CHUNK_UNROLL, intra, 0)

    def inter(grp, st):
        for j in range(CHUNK_UNROLL):
            ci = grp * CHUNK_UNROLL + j
            r0 = pl.multiple_of(ci * c, c)
            o = oi_s[pl.ds(r0, c), :] + lax.dot_general(
                qd_s[pl.ds(r0, c), :], st.astype(BF16), (((1,), (1,)), ((), ())),
                preferred_element_type=F32)
            st = st * eb_s[pl.ds(ci, 1), :] + m_s[ci]
            ms = jnp.mean(o * o, axis=-1, keepdims=True)
            y = o * lax.rsqrt(ms + NORM_EPS) * nw_ref[...] * _silu(gate_s[pl.ds(r0, c), :])
            o_ref[0, pl.ds(r0, c), :] = y.astype(o_ref.dtype)
        return st

    lax.fori_loop(0, nck // CHUNK_UNROLL, inter, jnp.zeros((GLA_DV, dk), F32))


def _split_dot_nt_left(tri_bf16, x):
    hi = x.astype(BF16)
    lo = (x - hi.astype(F32)).astype(BF16)
    return (jnp.dot(tri_bf16, hi, preferred_element_type=F32)
            + jnp.dot(tri_bf16, lo, preferred_element_type=F32))


def _gla_call(xb, w, wlr, wgu, bg, nw, tri):
    bsz, s, d = xb.shape
    return pl.pallas_call(
        _gla_kernel,
        grid=(bsz, GLA_HEADS),
        in_specs=[pl.BlockSpec((1, s, d), lambda b, h: (b, 0, 0)),
                  pl.BlockSpec((1, d, 4 * GLA_DK), lambda b, h: (h, 0, 0)),
                  pl.BlockSpec(wlr.shape, lambda b, h: (0, 0)),
                  pl.BlockSpec((1, LANES, GLA_DK), lambda b, h: (h, 0, 0)),
                  pl.BlockSpec((1, 1, GLA_DK), lambda b, h: (h, 0, 0)),
                  pl.BlockSpec(nw.shape, lambda b, h: (0, 0)),
                  pl.BlockSpec(tri.shape, lambda b, h: (0, 0))],
        out_specs=pl.BlockSpec((1, s, GLA_DV), lambda b, h: (b, 0, h)),
        out_shape=jax.ShapeDtypeStruct((bsz, s, GLA_HEADS * GLA_DV), BF16),
        scratch_shapes=[pltpu.VMEM((s, GLA_DK), F32)] * 6
        + [pltpu.VMEM((s, GLA_DK), BF16), pltpu.VMEM((s // GLA_CHUNK, GLA_DV, GLA_DK), F32),
           pltpu.VMEM((s // GLA_CHUNK, GLA_DK), F32)],
        compiler_params=pltpu.CompilerParams(dimension_semantics=("arbitrary", "arbitrary"),
                                             vmem_limit_bytes=VMEM_LIMIT),
        name="mixer_gla",
    )(xb, w, wlr, wgu, bg, nw, tri)


def _gdn_kernel(xb_ref, w_ref, wbd_ref, cw_ref, hp_ref, nw_ref, tri_ref, o_ref,
                q_s, k_s, v_s, beta_s, g_s, gate_s, u_s, wq_s, kd_s, attn_s, egl_s, st_s):
    s = xb_ref.shape[1]
    c = GDN_CHUNK
    dk = GDN_DK
    dv = GDN_DV
    nck = s // c
    row8 = lax.broadcasted_iota(jnp.int32, (SUBLANES, 3 * dk), 0)
    row = lax.broadcasted_iota(jnp.int32, (c, c), 0)
    col = lax.broadcasted_iota(jnp.int32, (c, c), 1)
    eye = (row == col).astype(F32)
    tri = tri_ref[...]
    avg = jnp.full((c, dk), 1.0 / dk, BF16)

    def head(h, carry):
        def proj(t, tail):
            us = range(2)
            r0 = [pl.multiple_of((2 * t + i) * ROW_TILE, ROW_TILE) for i in us]
            x = [xb_ref[0, pl.ds(r0[i], ROW_TILE), :] for i in us]
            p = [jnp.dot(x[i], w_ref[h], preferred_element_type=F32) for i in us]
            bd = [jnp.dot(x[i], wbd_ref[h], preferred_element_type=F32) for i in us]
            u = [p[i][:, :3 * dk] for i in us]
            tails = [tail, u[0][ROW_TILE - SUBLANES:, :]]
            qkv = [_silu(_causal_conv(u[i], tails[i], cw_ref.at[h], row8)) for i in us]
            qq = [qkv[i][:, :dk] for i in us]
            kk = [qkv[i][:, dk:2 * dk] for i in us]
            qn = [lax.rsqrt(jnp.sum(qq[i] * qq[i], axis=-1, keepdims=True) + NORM_EPS) * dk ** -0.5 for i in us]
            kn = [lax.rsqrt(jnp.sum(kk[i] * kk[i], axis=-1, keepdims=True) + NORM_EPS) for i in us]
            for i in us:
                rows = pl.ds(r0[i], ROW_TILE)
                q_s[rows, :] = qq[i] * qn[i]
                k_s[rows, :] = kk[i] * kn[i]
                v_s[rows, :] = qkv[i][:, 2 * dk:]
                gate_s[h, rows, :] = p[i][:, 3 * dk:]
                beta = jax.nn.sigmoid(bd[i][:, 0:1])
                g = -jnp.exp(hp_ref[h, 0:1, :]) * jax.nn.softplus(bd[i][:, 1:2] + hp_ref[h, 1:2, :])
                beta_s[rows, :] = jnp.broadcast_to(beta, (ROW_TILE, dk))
                g_s[rows, :] = g
            return u[1][ROW_TILE - SUBLANES:, :]

        lax.fori_loop(0, s // (2 * ROW_TILE), proj, jnp.zeros((SUBLANES, 3 * dk), F32))

        def intra(grp, c1):
            js = range(GDN_UNROLL)
            ci = [grp * GDN_UNROLL + j for j in js]
            r0 = [pl.multiple_of(ci[j] * c, c) for j in js]
            q = [q_s[pl.ds(r0[j], c), :] for j in js]
            k = [k_s[pl.ds(r0[j], c), :] for j in js]
            v = [v_s[pl.ds(r0[j], c), :] for j in js]
            beta = [beta_s[pl.ds(r0[j], c), :] for j in js]
            gc = [_split_dot_nt_left(tri, g_s[pl.ds(r0[j], c), :]) for j in js]
            gc_row = [_split_dot_nt(avg, gc[j]) for j in js]
            dec = [jnp.exp(jnp.minimum(gc[j][:, :c] - gc_row[j], 0.0)) for j in js]
            kb = [k[j] * beta[j] for j in js]
            qk = [_bdot_nt(jnp.concatenate([kb[j], q[j]], axis=0), k[j]) for j in js]
            a = [jnp.where(col < row, qk[j][:c] * dec[j], 0.0) for j in js]
            attn = [jnp.where(col <= row, qk[j][c:] * dec[j], 0.0) for j in js]
            tinv = [eye - a[j] for j in js]
            ak = a
            n = 1
            while n < c // 2:
                ak = [_bdot(ak[j], ak[j]) for j in js]
                tinv = [tinv[j] + _bdot(tinv[j], ak[j]) for j in js]
                n *= 2
            egc = [jnp.exp(gc[j]) for j in js]
            uw = [_bdot(tinv[j], jnp.concatenate([v[j] * beta[j], kb[j] * egc[j]], axis=1))
                  for j in js]
            for j in js:
                gl = gc[j][c - 1:c, :]
                u_s[h, pl.ds(r0[j], c), :] = uw[j][:, :dv]
                wq_s[h, ci[j]] = jnp.concatenate([uw[j][:, dv:], q[j] * egc[j]], axis=0).astype(BF16)
                kd_s[h, pl.ds(r0[j], c), :] = (k[j] * jnp.exp(gl - gc[j])).astype(BF16)
                attn_s[h, pl.ds(r0[j], c), :] = attn[j].astype(BF16)
                egl_s[h, pl.ds(ci[j], 1), :] = jnp.exp(gl)
            return c1

        lax.fori_loop(0, nck // GDN_UNROLL, intra, 0)
        return carry

    lax.fori_loop(0, GDN_HEADS, head, 0)

    st_s[...] = jnp.zeros_like(st_s)

    def inter(ci, c2):
        r0 = pl.multiple_of(ci * c, c)
        hs = range(GDN_HEADS)
        tn = (((0,), (0,)), ((), ()))
        st = [st_s[h] for h in hs]
        ws_qs = [jnp.dot(wq_s[h, ci], st[h].astype(BF16), preferred_element_type=F32) for h in hs]
        v_new = [(u_s[h, pl.ds(r0, c), :] - ws_qs[h][:c]).astype(BF16) for h in hs]
        o = [ws_qs[h][c:] + jnp.dot(attn_s[h, pl.ds(r0, c), :], v_new[h], preferred_element_type=F32)
             for h in hs]
        kv = [lax.dot_general(kd_s[h, pl.ds(r0, c), :], v_new[h], tn, preferred_element_type=F32)
              for h in hs]
        for h in hs:
            st_s[h] = st[h] * egl_s[h, pl.ds(ci, 1), :] + kv[h]
            ms = jnp.mean(o[h] * o[h], axis=-1, keepdims=True)
            y = o[h] * lax.rsqrt(ms + NORM_EPS) * nw_ref[...] * _silu(gate_s[h, pl.ds(r0, c), :])
            o_ref[0, pl.ds(r0, c), h * dv:(h + 1) * dv] = y.astype(o_ref.dtype)
        return c2

    lax.fori_loop(0, nck, inter, 0)


def _gdn_call(xb, w, wbd, cw, hp, nw, tri):
    bsz, s, d = xb.shape
    nh = GDN_HEADS
    nck = s // GDN_CHUNK
    const = lambda a: pl.BlockSpec(a.shape, lambda b: (0,) * a.ndim)
    return pl.pallas_call(
        _gdn_kernel,
        grid=(bsz,),
        in_specs=[pl.BlockSpec((1, s, d), lambda b: (b, 0, 0)),
                  const(w), const(wbd), const(cw), const(hp), const(nw), const(tri)],
        out_specs=pl.BlockSpec((1, s, nh * GDN_DV), lambda b: (b, 0, 0)),
        out_shape=jax.ShapeDtypeStruct((bsz, s, nh * GDN_DV), BF16),
        scratch_shapes=[pltpu.VMEM((s, GDN_DK), F32)] * 5
        + [pltpu.VMEM((nh, s, GDN_DV), F32),
           pltpu.VMEM((nh, s, GDN_DV), F32),
           pltpu.VMEM((nh, nck, 2 * GDN_CHUNK, GDN_DK), BF16),
           pltpu.VMEM((nh, s, GDN_DK), BF16),
           pltpu.VMEM((nh, s, GDN_CHUNK), BF16),
           pltpu.VMEM((nh, nck, GDN_DK), F32),
           pltpu.VMEM((nh, GDN_DK, GDN_DV), F32)],
        compiler_params=pltpu.CompilerParams(dimension_semantics=("arbitrary",),
                                             vmem_limit_bytes=VMEM_LIMIT),
        name="mixer_gdn",
    )(xb, w, wbd, cw, hp, nw, tri)


def _merge_kernel(x_ref, xb_ref, ya_ref, yb_ref, yc_ref, yd_ref, wm_ref, wb_ref, wo_ref,
                  g_ref, b_ref, o_ref, ob_ref):
    xb = xb_ref[...]
    merged = None
    for n, y_ref in enumerate((ya_ref, yb_ref, yc_ref, yd_ref)):
        gate = jax.nn.sigmoid(jnp.dot(xb, wm_ref[:, n * D_MODEL:(n + 1) * D_MODEL],
                                      preferred_element_type=F32))
        term = gate * jnp.dot(y_ref[...], wb_ref[n], preferred_element_type=F32)
        merged = term if merged is None else merged + term
    h = _bdot(merged, wo_ref[...])
    y = _layer_norm(DN_ALPHA * x_ref[...] + h, g_ref[...], b_ref[...])
    o_ref[...] = y
    ob_ref[...] = y.astype(BF16)


def _merge_call(x, xb, ya, yb, yc, yd, wm, wb, wo, g, b):
    t, d = x.shape
    tm = 256
    tok = lambda width: pl.BlockSpec((tm, width), lambda i: (i, 0))
    const = lambda shape: pl.BlockSpec(shape, lambda i: (0,) * len(shape),
                                       pipeline_mode=pl.Buffered(1))
    return pl.pallas_call(
        _merge_kernel,
        grid=(t // tm,),
        in_specs=[tok(d), tok(d), tok(BRANCH_WIDTH), tok(BRANCH_WIDTH), tok(BRANCH_WIDTH),
                  tok(BRANCH_WIDTH), const(wm.shape), const(wb.shape), const(wo.shape),
                  const(g.shape), const(b.shape)],
        out_specs=[tok(d), tok(d)],
        out_shape=[jax.ShapeDtypeStruct((t, d), F32), jax.ShapeDtypeStruct((t, d), BF16)],
        compiler_params=pltpu.CompilerParams(dimension_semantics=("arbitrary",),
                                             vmem_limit_bytes=VMEM_LIMIT),
        name="merge",
    )(x, xb, ya, yb, yc, yd, wm, wb, wo, g, b)


def _route(scores, biased):
    s = [scores[e:e + 1, :] for e in range(N_EXPERTS)]
    b = [biased[e:e + 1, :] for e in range(N_EXPERTS)]
    gs = []
    for g in range(N_GROUPS):
        m = b[EXPERTS_PER_GROUP * g:EXPERTS_PER_GROUP * (g + 1)]
        best = None
        for i in range(EXPERTS_PER_GROUP):
            for j in range(i + 1, EXPERTS_PER_GROUP):
                ps = m[i] + m[j]
                best = ps if best is None else jnp.maximum(best, ps)
        gs.append(best)
    gidx = jnp.zeros_like(gs[0], dtype=jnp.int32)
    gval = gs[0]
    for g in range(1, N_GROUPS):
        take = gs[g] > gval
        gidx = jnp.where(take, g, gidx)
        gval = jnp.where(take, gs[g], gval)
    w = []
    for e in range(N_EXPERTS):
        g = e // EXPERTS_PER_GROUP
        beaten = jnp.zeros_like(gidx)
        for j in range(EXPERTS_PER_GROUP * g, EXPERTS_PER_GROUP * (g + 1)):
            if j == e:
                continue
            wins = (b[j] >= b[e]) if j < e else (b[j] > b[e])
            beaten = beaten + wins.astype(jnp.int32)
        sel = (gidx == g) & (beaten < 2)
        w.append(jnp.where(sel, s[e], 0.0))
    tot = w[0]
    for e in range(1, N_EXPERTS):
        tot = tot + w[e]
    return [w[e] / tot for e in range(N_EXPERTS)], gidx


def _moe_kernel(x_ref, xb_ref, wr_ref, rb_ref, su_ref, wgu_ref, wd_ref, g_ref, b_ref,
                o_ref, ob_ref, acc_ref, tr_ref, slotr_ref, slotc_ref, chl_ref, cnt_ref):
    g = pl.program_id(1)
    tm = x_ref.shape[0]
    cc = MOE_CHUNK

    @pl.when(g == 0)
    def _():
        logits = lax.dot_general(wr_ref[...], x_ref[...], (((1,), (1,)), ((), ())),
                                 precision=lax.Precision.HIGHEST, preferred_element_type=F32)
        scores = jax.nn.sigmoid(logits)
        rows, gidx = _route(scores, scores + rb_ref[...])
        member = [(gidx == k).astype(F32) for k in range(N_GROUPS)]
        mm = jnp.concatenate(member + [jnp.zeros((SUBLANES - N_GROUPS, tm), F32)], axis=0)
        rank = jnp.dot(mm.astype(BF16), su_ref[...], preferred_element_type=F32)
        slot = jnp.where(mm > 0.0, rank, -1.0)
        slotr_ref[...] = slot
        for k in range(N_GROUPS):
            cnt_ref[k] = jnp.sum(member[k]).astype(jnp.int32)
        tr_ref[...] = jnp.zeros_like(tr_ref)
        for k in range(N_EXPERTS):
            tr_ref[k:k + 1, :] = rows[k]
        tr_ref[N_EXPERTS:N_EXPERTS + SUBLANES, :] = slot
        tc = tr_ref[...].T
        hi = tc.astype(BF16)
        chl_ref[:, :LANES] = hi
        chl_ref[:, LANES:] = (tc - hi.astype(F32)).astype(BF16)
        for k in range(N_GROUPS):
            slotc_ref[k] = jnp.broadcast_to(tc[:, N_EXPERTS + k:N_EXPERTS + k + 1], (tm, LANES))
        acc_ref[...] = jnp.zeros_like(acc_ref)

    slot_row = slotr_ref[pl.ds(g, 1), :]

    def expert_pass(base, rows):
        slot_col = jnp.concatenate([slotc_ref[g]] * (rows // LANES), axis=1)
        lane = lax.broadcasted_iota(jnp.int32, (rows, LANES), 1)
        row_i = lax.broadcasted_iota(jnp.int32, (rows, tm), 0).astype(F32)
        col_i = lax.broadcasted_iota(jnp.int32, (tm, rows), 1).astype(F32)
        basef = base.astype(F32)
        gather = (slot_row == row_i + basef).astype(BF16)
        xg = jnp.dot(gather, xb_ref[...], preferred_element_type=F32).astype(BF16)
        cw2 = jnp.dot(gather, chl_ref[...], preferred_element_type=F32)
        cw = cw2[:, :LANES] + cw2[:, LANES:]
        yg = None
        for k in range(EXPERTS_PER_GROUP):
            hgu = jnp.dot(xg, wgu_ref[0, k], preferred_element_type=F32)
            h = _silu(hgu[:, :D_EXPERT]) * hgu[:, D_EXPERT:]
            ce = jnp.sum(jnp.where(lane == g * EXPERTS_PER_GROUP + k, cw, 0.0), axis=-1, keepdims=True)
            term = ce * _bdot(h, wd_ref[0, k])
            yg = term if yg is None else yg + term
        scatter = (slot_col == col_i + basef).astype(BF16)
        acc_ref[...] += jnp.dot(scatter, yg.astype(BF16), preferred_element_type=F32)

    half = cc // 2
    n = cnt_ref[g]
    rem = jnp.bitwise_and(n, cc - 1)
    small = jnp.logical_and(rem > 0, rem <= half).astype(jnp.int32)
    n_full = jnp.right_shift(n + cc - 1, MOE_CHUNK_LOG2) - small

    def full_pass(ci, carry):
        expert_pass(ci * cc, cc)
        return carry

    def half_pass(ci, carry):
        expert_pass(n_full * cc, half)
        return carry

    lax.fori_loop(0, n_full, full_pass, 0)
    lax.fori_loop(0, small, half_pass, 0)

    @pl.when(g == N_GROUPS - 1)
    def _():
        out = _layer_norm(DN_ALPHA * x_ref[...] + acc_ref[...], g_ref[...], b_ref[...])
        o_ref[...] = out
        ob_ref[...] = out.astype(BF16)


def _moe_call(x, xb, wr, rb, su, wgu, wd, g, b):
    t, d = x.shape
    tm = MOE_TILE
    epg = EXPERTS_PER_GROUP
    tok = lambda mode=None: pl.BlockSpec((tm, d), lambda i, k: (i, 0), pipeline_mode=mode)
    const = lambda shape, mode=None: pl.BlockSpec(shape, lambda i, k: (0,) * len(shape),
                                                  pipeline_mode=mode)
    return pl.pallas_call(
        _moe_kernel,
        grid=(t // tm, N_GROUPS),
        in_specs=[tok(pl.Buffered(1)), tok(pl.Buffered(1)), const(wr.shape), const(rb.shape),
                  const(su.shape, pl.Buffered(1)),
                  pl.BlockSpec((1, epg, d, 2 * D_EXPERT), lambda i, k: (k, 0, 0, 0)),
                  pl.BlockSpec((1, epg, D_EXPERT, d), lambda i, k: (k, 0, 0, 0)),
                  const(g.shape), const(b.shape)],
        out_specs=[tok(), tok()],
        out_shape=[jax.ShapeDtypeStruct((t, d), F32), jax.ShapeDtypeStruct((t, d), BF16)],
        scratch_shapes=[pltpu.VMEM((tm, d), F32),
                        pltpu.VMEM((LANES, tm), F32),
                        pltpu.VMEM((SUBLANES, tm), F32),
                        pltpu.VMEM((N_GROUPS, tm, LANES), F32),
                        pltpu.VMEM((tm, 2 * LANES), BF16),
                        pltpu.SMEM((N_GROUPS,), jnp.int32)],
        compiler_params=pltpu.CompilerParams(dimension_semantics=("arbitrary", "arbitrary"),
                                             vmem_limit_bytes=VMEM_LIMIT),
        name="moe",
    )(x, xb, wr, rb, su, wgu, wd, g, b)


def _block_diag(w):
    g, n, _ = w.shape
    eye = jnp.eye(g, dtype=w.dtype)
    return (eye[:, None, :, None] * w[:, :, None, :]).reshape(g * n, g * n)


def _per_head(w, off, n_heads, width, parts):
    cols = [w[:, off + p * n_heads * width: off + (p + 1) * n_heads * width]
            .reshape(w.shape[0], n_heads, width) for p in parts]
    return jnp.transpose(jnp.concatenate(cols, axis=2), (1, 0, 2))


def _row(v):
    return v.reshape(1, -1).astype(F32)


def kernel(x, w_in, conv_a_w, conv_a_b, rg_w_a, rg_b_a, rg_w_x, rg_b_x, rg_lambda, gdn_conv_w, gdn_a_log, gdn_dt_bias, gdn_norm_w, gla_w_gate_up, gla_b_gate, gla_norm_w, w_branch, w_out, ln1_g, ln1_b, w_router, router_bias, w_gate, w_up, w_down, ln2_g, ln2_b):
    bsz, s, d = x.shape
    t = bsz * s
    c = GDN_CHUNK
    ii = jnp.arange(SB_BLOCK)
    u2 = jnp.concatenate([(ii[:, None] >= ii[None, :]).astype(BF16),
                          jnp.ones((SB_BLOCK, SB_BLOCK), BF16)], axis=1)
    u2 = jnp.concatenate([u2, u2], axis=0)
    jj = jnp.arange(c)
    tri = (jj[:, None] >= jj[None, :]).astype(BF16)
    wr = w_router.astype(F32).T
    rb = router_bias.astype(F32).reshape(N_EXPERTS, 1)
    kk = jnp.arange(MOE_TILE)
    su = (kk[:, None] < kk[None, :]).astype(BF16)

    xf = x.reshape(t, d)
    xb = xf.astype(BF16)
    for l in range(DEPTH):
        wl = w_in[l]
        wlb = wl.astype(BF16)
        xb3 = xb.reshape(bsz, s, d)

        wg = jnp.concatenate([_block_diag(rg_w_a[l]), _block_diag(rg_w_x[l])], axis=1).astype(BF16)
        bg = jnp.concatenate([rg_b_a[l], rg_b_x[l]]).reshape(1, -1).astype(F32)
        ya = _lru_call(xb3, wlb[:, _OFF_A:_OFF_A + LRU_WIDTH], conv_a_w[l].astype(F32),
                       _row(conv_a_b[l]), wg, bg, _row(rg_lambda[l]))

        w_gdn = jnp.concatenate([_per_head(wlb, _OFF_GDN_QKV, GDN_HEADS, GDN_DK, (0, 1, 2)),
                                 _per_head(wlb, _OFF_GDN_GATE, GDN_HEADS, GDN_DV, (0,))], axis=2)
        wbd = jnp.stack([wlb[:, _OFF_GDN_BETA:_OFF_GDN_BETA + GDN_HEADS],
                         wlb[:, _OFF_GDN_DECAY:_OFF_GDN_DECAY + GDN_HEADS]], axis=2)
        wbd = jnp.pad(jnp.transpose(wbd, (1, 0, 2)), ((0, 0), (0, 0), (0, LANES - 2)))
        cw_gdn = jnp.transpose(gdn_conv_w[l].astype(F32).reshape(CONV_WIDTH, 3, GDN_HEADS, GDN_DK),
                               (2, 0, 1, 3)).reshape(GDN_HEADS, CONV_WIDTH, 3 * GDN_DK)
        hp = jnp.zeros((GDN_HEADS, SUBLANES, GDN_DK), F32)
        hp = hp.at[:, 0, :].set(gdn_a_log[l].astype(F32)[:, None])
        hp = hp.at[:, 1, :].set(gdn_dt_bias[l].astype(F32)[:, None])
        yb = _gdn_call(xb3, w_gdn, wbd, cw_gdn, hp, _row(gdn_norm_w[l]), tri)

        yc = _sb_call(xb3, wlb[:, _OFF_SB:_OFF_SB + 3 * SB_HEADS * SB_DH], u2)

        q_off = _OFF_GLA_QKV
        w_gla = jnp.concatenate([
            _per_head(wlb, q_off, GLA_HEADS, GLA_DK, (0, 1)),
            _per_head(wlb, q_off + 2 * GLA_HEADS * GLA_DK, GLA_HEADS, GLA_DV, (0,)),
            _per_head(wlb, _OFF_GLA_GATE, GLA_HEADS, GLA_DV, (0,))], axis=2)
        wlr = jnp.pad(wlb[:, _OFF_GLA_LR:_OFF_GLA_LR + GLA_GATE_RANK],
                      ((0, 0), (0, LANES - GLA_GATE_RANK)))
        wgu = jnp.pad(jnp.transpose(gla_w_gate_up[l].reshape(GLA_GATE_RANK, GLA_HEADS, GLA_DK), (1, 0, 2)),
                      ((0, 0), (0, LANES - GLA_GATE_RANK), (0, 0))).astype(BF16)
        bgl = gla_b_gate[l].astype(F32).reshape(GLA_HEADS, 1, GLA_DK)
        yd = _gla_call(xb3, w_gla, wlr, wgu, bgl, _row(gla_norm_w[l]), tri)

        xf, xb = _merge_call(xf, xb, ya.reshape(t, -1), yb.reshape(t, -1), yc.reshape(t, -1),
                             yd.reshape(t, -1), wlb[:, _OFF_MERGE:], w_branch[l].astype(BF16),
                             w_out[l].astype(BF16), _row(ln1_g[l]), _row(ln1_b[l]))

        w_gu = jnp.concatenate([w_gate[l].astype(BF16), w_up[l].astype(BF16)], axis=2)
        w_gu = w_gu.reshape(N_GROUPS, EXPERTS_PER_GROUP, d, 2 * D_EXPERT)
        w_dn = w_down[l].astype(BF16).reshape(N_GROUPS, EXPERTS_PER_GROUP, D_EXPERT, d)
        xf, xb = _moe_call(xf, xb, wr, rb, su, w_gu, w_dn, _row(ln2_g[l]), _row(ln2_b[l]))
    return xf.reshape(bsz, s, d)
```

```python
import functools

import jax
import jax.numpy as jnp
from jax import lax
from jax.experimental import pallas as pl
from jax.experimental.pallas import tpu as pltpu

F32 = jnp.float32
BF16 = jnp.bfloat16

D_MODEL = 1024
DEPTH = 4
LRU_WIDTH = 512
LRU_BLOCKS = 8
LRU_C = 8.0
CONV_WIDTH = 4
GDN_HEADS = 4
GDN_DK = 128
GDN_DV = 128
GDN_CHUNK = 64
SB_HEADS = 8
SB_DH = 64
SB_BLOCK = 128
GLA_HEADS = 4
GLA_DK = 128
GLA_DV = 128
GLA_GATE_RANK = 16
GLA_TAU = 16.0
GLA_CHUNK = 64
N_BRANCHES = 4
BRANCH_WIDTH = 512
N_EXPERTS = 16
N_GROUPS = 4
EXPERTS_PER_GROUP = 4
D_EXPERT = 512
LN_EPS = 1e-5
NORM_EPS = 1e-6
DN_ALPHA = (2 * DEPTH) ** 0.25
LOG2E = 1.4426950408889634
SB_UNDERFLOW_LOG2 = 150.0

_OFF_A = 0
_OFF_GDN_QKV = _OFF_A + LRU_WIDTH
_OFF_GDN_BETA = _OFF_GDN_QKV + 3 * GDN_HEADS * GDN_DK
_OFF_GDN_DECAY = _OFF_GDN_BETA + GDN_HEADS
_OFF_GDN_GATE = _OFF_GDN_DECAY + GDN_HEADS
_OFF_SB = _OFF_GDN_GATE + GDN_HEADS * GDN_DV
_OFF_GLA_QKV = _OFF_SB + 3 * SB_HEADS * SB_DH
_OFF_GLA_LR = _OFF_GLA_QKV + GLA_HEADS * (2 * GLA_DK + GLA_DV)
_OFF_GLA_GATE = _OFF_GLA_LR + GLA_GATE_RANK
_OFF_MERGE = _OFF_GLA_GATE + GLA_HEADS * GLA_DV

LANES = 128
SUBLANES = 8
ROW_TILE = 256
LRU_SUBTILES = 2
MERGE_TILE = 512
CHUNK_UNROLL = 8
GDN_UNROLL = 16
MOE_TILE = 1024
MOE_CHUNK_LOG2 = 8
MOE_CHUNK = 1 << MOE_CHUNK_LOG2
VMEM_LIMIT = 56 * 1024 * 1024


def _bdot(a, b):
    return jnp.dot(a.astype(BF16), b.astype(BF16), preferred_element_type=F32)


def _bdot_nt(a, b):
    return lax.dot_general(a.astype(BF16), b.astype(BF16), (((1,), (1,)), ((), ())),
                           preferred_element_type=F32)


def _bdot_tn(a, b):
    return lax.dot_general(a.astype(BF16), b.astype(BF16), (((0,), (0,)), ((), ())),
                           preferred_element_type=F32)


def _split_dot(a, b_bf16):
    hi = a.astype(BF16)
    lo = (a - hi.astype(F32)).astype(BF16)
    return (jnp.dot(hi, b_bf16, preferred_element_type=F32)
            + jnp.dot(lo, b_bf16, preferred_element_type=F32))


def _split_dot_nt(a_bf16, b):
    hi = b.astype(BF16)
    lo = (b - hi.astype(F32)).astype(BF16)
    dn = (((1,), (1,)), ((), ()))
    return (lax.dot_general(a_bf16, hi, dn, preferred_element_type=F32)
            + lax.dot_general(a_bf16, lo, dn, preferred_element_type=F32))


def _silu(x):
    return x * jax.nn.sigmoid(x)


def _layer_norm(x, g, b):
    mu = jnp.mean(x, axis=-1, keepdims=True)
    xc = x - mu
    var = jnp.mean(xc * xc, axis=-1, keepdims=True)
    return xc * lax.rsqrt(var + LN_EPS) * g + b


def _shift_rows(u, tail, s, row8):
    ur = pltpu.roll(u, s, 0)
    top = jnp.where(row8 >= s, ur[:SUBLANES], pltpu.roll(tail, s, 0))
    return jnp.concatenate([top, ur[SUBLANES:]], axis=0)


def _causal_conv(u, tail, cw_ref, row8):
    y = u * cw_ref[CONV_WIDTH - 1:CONV_WIDTH, :]
    for s in range(1, CONV_WIDTH):
        k = CONV_WIDTH - 1 - s
        y = y + _shift_rows(u, tail, s, row8) * cw_ref[k:k + 1, :]
    return y


def _lru_kernel(xb_ref, w_ref, cw_ref, cb_ref, wg_ref, bg_ref, lam_ref, o_ref, tail_ref, h_ref):
    t = pl.program_id(1)
    tt = ROW_TILE
    nsub = xb_ref.shape[1] // tt
    subs = range(nsub)
    w = LRU_WIDTH

    @pl.when(t == 0)
    def _():
        tail_ref[...] = jnp.zeros_like(tail_ref)
        h_ref[...] = jnp.zeros_like(h_ref)

    u = [jnp.dot(xb_ref[0, i * tt:(i + 1) * tt, :], w_ref[...], preferred_element_type=F32)
         for i in subs]
    row = lax.broadcasted_iota(jnp.int32, (tt, w), 0)
    row8 = lax.broadcasted_iota(jnp.int32, (SUBLANES, w), 0)
    tails = [tail_ref[...]] + [u[i][tt - SUBLANES:, :] for i in subs[:-1]]
    xc = [_causal_conv(u[i], tails[i], cw_ref, row8) + cb_ref[...] for i in subs]
    tail_ref[...] = u[nsub - 1][tt - SUBLANES:, :]

    gates = [_bdot(xc[i], wg_ref[...]) + bg_ref[...] for i in subs]
    sp = jax.nn.softplus(-lam_ref[...])
    log_a = [(-LRU_C) * jax.nn.sigmoid(gates[i][:, :w]) * sp for i in subs]
    a = [jnp.exp(log_a[i]) for i in subs]
    mult = [jnp.sqrt(-jnp.tanh(log_a[i]) * (a[i] * a[i] + 1.0)) for i in subs]
    mult[0] = jnp.where(row + t == 0, 1.0, mult[0])
    b = [mult[i] * jax.nn.sigmoid(gates[i][:, w:]) * xc[i] for i in subs]

    d = 1
    while d < SUBLANES:
        keep = jnp.bitwise_and(row, SUBLANES - 1) >= d
        a_sh = [jnp.where(keep, pltpu.roll(a[i], d, 0), 1.0) for i in subs]
        b_sh = [jnp.where(keep, pltpu.roll(b[i], d, 0), 0.0) for i in subs]
        b = [a[i] * b_sh[i] + b[i] for i in subs]
        a = [a[i] * a_sh[i] for i in subs]
        d *= 2
    carry = h_ref[...]
    for i in subs:
        hs = []
        for gi in range(tt // SUBLANES):
            rows = slice(gi * SUBLANES, (gi + 1) * SUBLANES)
            hg = a[i][rows, :] * carry + b[i][rows, :]
            carry = jnp.broadcast_to(hg[SUBLANES - 1:SUBLANES, :], (SUBLANES, w))
            hs.append(hg)
        o_ref[0, i * tt:(i + 1) * tt, :] = jnp.concatenate(hs, axis=0).astype(o_ref.dtype)
    h_ref[...] = carry


def _lru_call(xb, w, cw, cb, wg, bg, lam):
    bsz, s, d = xb.shape
    tt = LRU_SUBTILES * ROW_TILE
    const = lambda shape: pl.BlockSpec(shape, lambda b, t: (0,) * len(shape))
    return pl.pallas_call(
        _lru_kernel,
        grid=(bsz, s // tt),
        in_specs=[pl.BlockSpec((1, tt, d), lambda b, t: (b, t, 0)),
                  const(w.shape), const(cw.shape), const(cb.shape), const(wg.shape),
                  const(bg.shape), const(lam.shape)],
        out_specs=pl.BlockSpec((1, tt, LRU_WIDTH), lambda b, t: (b, t, 0)),
        out_shape=jax.ShapeDtypeStruct((bsz, s, LRU_WIDTH), BF16),
        scratch_shapes=[pltpu.VMEM((SUBLANES, LRU_WIDTH), F32), pltpu.VMEM((SUBLANES, LRU_WIDTH), F32)],
        compiler_params=pltpu.CompilerParams(dimension_semantics=("arbitrary", "arbitrary"),
                                             vmem_limit_bytes=VMEM_LIMIT),
        name="mixer_lru",
    )(xb, w, cw, cb, wg, bg, lam)


def _sb_kernel(xb_ref, w_ref, u2_ref, o_ref, q_s, k_s, v_s, acc_s, run_s):
    s = xb_ref.shape[1]
    blk = SB_BLOCK
    dh = SB_DH
    nh = SB_HEADS
    hd = nh * dh

    def colmax(x):
        m = x[:SUBLANES]
        for gi in range(1, ROW_TILE // SUBLANES):
            m = jnp.maximum(m, x[gi * SUBLANES:(gi + 1) * SUBLANES])
        return m

    def proj(t, carry):
        cq, ck = carry
        r0 = pl.multiple_of(t * ROW_TILE, ROW_TILE)
        qkv = jnp.dot(xb_ref[0, pl.ds(r0, ROW_TILE), :], w_ref[...], preferred_element_type=F32)
        qb = (qkv[:, :hd] * (dh ** -0.5 * LOG2E)).astype(BF16)
        kb = qkv[:, hd:2 * hd].astype(BF16)
        for h in range(nh):
            q_s[h, pl.ds(r0, ROW_TILE), :] = qb[:, h * dh:(h + 1) * dh]
            k_s[h, pl.ds(r0, ROW_TILE), :] = kb[:, h * dh:(h + 1) * dh]
            v_s[h, pl.ds(r0, ROW_TILE), :] = qkv[:, 2 * hd + h * dh:2 * hd + (h + 1) * dh].astype(BF16)
        qf = qb.astype(F32)
        kf = kb.astype(F32)
        return jnp.maximum(cq, colmax(qf * qf)), jnp.maximum(ck, colmax(kf * kf))

    zeros8 = jnp.zeros((SUBLANES, hd), F32)
    cq, ck = lax.fori_loop(0, s // ROW_TILE, proj, (zeros8, zeros8))

    cq = jnp.max(cq, axis=0, keepdims=True)
    ck = jnp.max(ck, axis=0, keepdims=True)
    sq = None
    sk = None
    for h in range(nh):
        a_ = jnp.sum(cq[:, h * dh:(h + 1) * dh], axis=-1, keepdims=True)
        b_ = jnp.sum(ck[:, h * dh:(h + 1) * dh], axis=-1, keepdims=True)
        sq = a_ if sq is None else jnp.maximum(sq, a_)
        sk = b_ if sk is None else jnp.maximum(sk, b_)
    stop_below = jnp.max(-(SB_UNDERFLOW_LOG2 + 1.001 * jnp.sqrt(sq * sk)))

    row = lax.broadcasted_iota(jnp.int32, (blk, blk), 0)
    col = lax.broadcasted_iota(jnp.int32, (blk, blk), 1)
    u2 = u2_ref[...]

    def max_run():
        m = run_s[0]
        for h in range(1, nh):
            m = jnp.maximum(m, run_s[h])
        return jnp.max(m)

    def qloop(qi, c):
        q0 = pl.multiple_of(qi * blk, blk)
        acc_s[...] = jnp.zeros_like(acc_s)
        run_s[...] = jnp.zeros_like(run_s)

        def blocks(k0s, diagonal):
            ch = [(h, j) for j in range(len(k0s)) for h in range(nh)]
            nt = (((1,), (1,)), ((), ()))
            z = {c_: lax.dot_general(q_s[c_[0], pl.ds(q0, blk), :], k_s[c_[0], pl.ds(k0s[c_[1]], blk), :],
                                     nt, preferred_element_type=F32) for c_ in ch}
            nz = {c_: -z[c_] for c_ in ch}
            lk = {c_: jnp.minimum(nz[c_], 0.0) - jnp.log2(1.0 + jnp.exp2(jnp.minimum(z[c_], nz[c_])))
                  for c_ in ch}
            lk = {c_: jnp.where(col < row, lk[c_], 0.0) if diagonal[c_[1]] else lk[c_] for c_ in ch}
            hi = {c_: lk[c_].astype(BF16) for c_ in ch}
            lo = {c_: (lk[c_] - hi[c_].astype(F32)).astype(BF16) for c_ in ch}
            cs2 = {c_: jnp.dot(jnp.concatenate([hi[c_], lo[c_]], axis=1), u2, preferred_element_type=F32)
                   for c_ in ch}
            run = {}
            for h in range(nh):
                r = run_s[h]
                for j in range(len(k0s)):
                    run[(h, j)] = r
                    r = r + cs2[(h, j)][:, blk:]
                run_s[h] = r
            p = {c_: jnp.exp2(z[c_] + cs2[c_][:, :blk] + run[c_]) for c_ in ch}
            p = {c_: jnp.where(col < row, p[c_], 0.0) if diagonal[c_[1]] else p[c_] for c_ in ch}
            pv = {c_: jnp.dot(p[c_].astype(BF16), v_s[c_[0], pl.ds(k0s[c_[1]], blk), :],
                              preferred_element_type=F32) for c_ in ch}
            for h in range(nh):
                tot = pv[(h, 0)]
                for j in range(1, len(k0s)):
                    tot = tot + pv[(h, j)]
                acc_s[h] += tot

        blocks([q0], (True,))
        npairs = jnp.right_shift(qi, 1)

        def more(c2):
            return jnp.logical_and(c2[0] < npairs, c2[1] > stop_below)

        def pair(c2):
            i = c2[0]
            ka = pl.multiple_of((qi - 1 - 2 * i) * blk, blk)
            kb = pl.multiple_of((qi - 2 - 2 * i) * blk, blk)
            blocks([ka, kb], (False, False))
            return i + 1, max_run()

        _, top = lax.while_loop(more, pair, (jnp.int32(0), max_run()))

        def last(i, c2):
            blocks([0], (False,))
            return c2

        live = jnp.logical_and(jnp.bitwise_and(qi, 1) == 1, top > stop_below)
        lax.fori_loop(0, live.astype(jnp.int32), last, 0)
        for h in range(nh):
            o_ref[0, pl.ds(q0, blk), h * dh:(h + 1) * dh] = acc_s[h].astype(o_ref.dtype)
        return c

    lax.fori_loop(0, s // blk, qloop, 0)


def _sb_call(xb, w, u2):
    bsz, s, d = xb.shape
    return pl.pallas_call(
        _sb_kernel,
        grid=(bsz,),
        in_specs=[pl.BlockSpec((1, s, d), lambda b: (b, 0, 0)),
                  pl.BlockSpec(w.shape, lambda b: (0, 0)),
                  pl.BlockSpec(u2.shape, lambda b: (0, 0))],
        out_specs=pl.BlockSpec((1, s, SB_HEADS * SB_DH), lambda b: (b, 0, 0)),
        out_shape=jax.ShapeDtypeStruct((bsz, s, SB_HEADS * SB_DH), BF16),
        scratch_shapes=[pltpu.VMEM((SB_HEADS, s, SB_DH), BF16)] * 3
        + [pltpu.VMEM((SB_HEADS, SB_BLOCK, SB_DH), F32), pltpu.VMEM((SB_HEADS, SB_BLOCK, SB_BLOCK), F32)],
        compiler_params=pltpu.CompilerParams(dimension_semantics=("arbitrary",),
                                             vmem_limit_bytes=VMEM_LIMIT),
        name="mixer_sb",
    )(xb, w, u2)


def _gla_kernel(xb_ref, w_ref, wlr_ref, wgu_ref, bg_ref, nw_ref, tri_ref, o_ref,
                q_s, k_s, v_s, g_s, gate_s, oi_s, qd_s, m_s, eb_s):
    s = xb_ref.shape[1]
    c = GLA_CHUNK
    dk = GLA_DK
    nck = s // c

    def proj(t, carry):
        us = range(2)
        r0 = [pl.multiple_of((2 * t + i) * ROW_TILE, ROW_TILE) for i in us]
        x = [xb_ref[0, pl.ds(r0[i], ROW_TILE), :] for i in us]
        lr = [jnp.dot(x[i], wlr_ref[...], preferred_element_type=F32) for i in us]
        p = [jnp.dot(x[i], w_ref[0], preferred_element_type=F32) for i in us]
        gpre = [_bdot(lr[i], wgu_ref[0]) + bg_ref[0] for i in us]
        for i in us:
            rows = pl.ds(r0[i], ROW_TILE)
            g_s[rows, :] = jax.nn.log_sigmoid(gpre[i]) * (1.0 / GLA_TAU)
            q_s[rows, :] = p[i][:, :dk] * dk ** -0.5
            k_s[rows, :] = p[i][:, dk:2 * dk]
            v_s[rows, :] = p[i][:, 2 * dk:3 * dk]
            gate_s[rows, :] = p[i][:, 3 * dk:]
        return carry

    lax.fori_loop(0, s // (2 * ROW_TILE), proj, 0)

    row = lax.broadcasted_iota(jnp.int32, (c, c), 0)
    col = lax.broadcasted_iota(jnp.int32, (c, c), 1)
    tri = tri_ref[...]

    def intra(grp, carry):
        u = range(CHUNK_UNROLL)
        ci = [grp * CHUNK_UNROLL + j for j in u]
        r0 = [pl.multiple_of(ci[j] * c, c) for j in u]
        q = [q_s[pl.ds(r0[j], c), :] for j in u]
        k = [k_s[pl.ds(r0[j], c), :] for j in u]
        v = [v_s[pl.ds(r0[j], c), :] for j in u]
        bc = [_split_dot_nt_left(tri, g_s[pl.ds(r0[j], c), :]) for j in u]
        blast = [bc[j][c - 1:c, :] for j in u]
        qd = [q[j] * jnp.exp(bc[j]) for j in u]
        ki = [k[j] * jnp.exp(-bc[j]) for j in u]
        kd = [k[j] * jnp.exp(blast[j] - bc[j]) for j in u]
        attn = [jnp.where(col <= row, _bdot_nt(qd[j], ki[j]), 0.0) for j in u]
        oi = [_bdot(attn[j], v[j]) for j in u]
        m = [_bdot_tn(v[j], kd[j]) for j in u]
        for j in u:
            oi_s[pl.ds(r0[j], c), :] = oi[j]
            qd_s[pl.ds(r0[j], c), :] = qd[j].astype(BF16)
            m_s[ci[j]] = m[j]
            eb_s[pl.ds(ci[j], 1), :] = jnp.exp(blast[j])
        return carry

    lax.fori_loop(0, nck // CHUNK_UNROLL, intra, 0)

    def inter(grp, st):
        for j in range(CHUNK_UNROLL):
            ci = grp * CHUNK_UNROLL + j
            r0 = pl.multiple_of(ci * c, c)
            o = oi_s[pl.ds(r0, c), :] + lax.dot_general(
                qd_s[pl.ds(r0, c), :], st.astype(BF16), (((1,), (1,)), ((), ())),
                preferred_element_type=F32)
            st = st * eb_s[pl.ds(ci, 1), :] + m_s[ci]
            ms = jnp.mean(o * o, axis=-1, keepdims=True)
            y = o * lax.rsqrt(ms + NORM_EPS) * nw_ref[...] * _silu(gate_s[pl.ds(r0, c), :])
            o_ref[0, pl.ds(r0, c), :] = y.astype(o_ref.dtype)
        return st

    lax.fori_loop(0, nck // CHUNK_UNROLL, inter, jnp.zeros((GLA_DV, dk), F32))


def _split_dot_nt_left(tri_bf16, x):
    hi = x.astype(BF16)
    lo = (x - hi.astype(F32)).astype(BF16)
    return (jnp.dot(tri_bf16, hi, preferred_element_type=F32)
            + jnp.dot(tri_bf16, lo, preferred_element_type=F32))


def _gla_call(xb, w, wlr, wgu, bg, nw, tri):
    bsz, s, d = xb.shape
    return pl.pallas_call(
        _gla_kernel,
        grid=(bsz, GLA_HEADS),
        in_specs=[pl.BlockSpec((1, s, d), lambda b, h: (b, 0, 0)),
                  pl.BlockSpec((1, d, 4 * GLA_DK), lambda b, h: (h, 0, 0)),
                  pl.BlockSpec(wlr.shape, lambda b, h: (0, 0)),
                  pl.BlockSpec((1, LANES, GLA_DK), lambda b, h: (h, 0, 0)),
                  pl.BlockSpec((1, 1, GLA_DK), lambda b, h: (h, 0, 0)),
                  pl.BlockSpec(nw.shape, lambda b, h: (0, 0)),
                  pl.BlockSpec(tri.shape, lambda b, h: (0, 0))],
        out_specs=pl.BlockSpec((1, s, GLA_DV), lambda b, h: (b, 0, h)),
        out_shape=jax.ShapeDtypeStruct((bsz, s, GLA_HEADS * GLA_DV), BF16),
        scratch_shapes=[pltpu.VMEM((s, GLA_DK), F32)] * 6
        + [pltpu.VMEM((s, GLA_DK), BF16), pltpu.VMEM((s // GLA_CHUNK, GLA_DV, GLA_DK), F32),
           pltpu.VMEM((s // GLA_CHUNK, GLA_DK), F32)],
        compiler_params=pltpu.CompilerParams(dimension_semantics=("arbitrary", "arbitrary"),
                                             vmem_limit_bytes=VMEM_LIMIT),
        name="mixer_gla",
    )(xb, w, wlr, wgu, bg, nw, tri)


def _gdn_kernel(xb_ref, w_ref, wbd_ref, cw_ref, hp_ref, nw_ref, tri_ref, o_ref,
                q_s, k_s, v_s, beta_s, g_s, gate_s, u_s, wq_s, kd_s, attn_s, egl_s, st_s):
    s = xb_ref.shape[1]
    c = GDN_CHUNK
    dk = GDN_DK
    dv = GDN_DV
    nck = s // c
    row8 = lax.broadcasted_iota(jnp.int32, (SUBLANES, 3 * dk), 0)
    row = lax.broadcasted_iota(jnp.int32, (c, c), 0)
    col = lax.broadcasted_iota(jnp.int32, (c, c), 1)
    eye = (row == col).astype(F32)
    tri = tri_ref[...]
    avg = jnp.full((c, dk), 1.0 / dk, BF16)

    def head(h, carry):
        def proj(t, tail):
            us = range(2)
            r0 = [pl.multiple_of((2 * t + i) * ROW_TILE, ROW_TILE) for i in us]
            x = [xb_ref[0, pl.ds(r0[i], ROW_TILE), :] for i in us]
            p = [jnp.dot(x[i], w_ref[h], preferred_element_type=F32) for i in us]
            bd = [jnp.dot(x[i], wbd_ref[h], preferred_element_type=F32) for i in us]
            u = [p[i][:, :3 * dk] for i in us]
            tails = [tail, u[0][ROW_TILE - SUBLANES:, :]]
            qkv = [_silu(_causal_conv(u[i], tails[i], cw_ref.at[h], row8)) for i in us]
            qq = [qkv[i][:, :dk] for i in us]
            kk = [qkv[i][:, dk:2 * dk] for i in us]
            qn = [lax.rsqrt(jnp.sum(qq[i] * qq[i], axis=-1, keepdims=True) + NORM_EPS) * dk ** -0.5 for i in us]
            kn = [lax.rsqrt(jnp.sum(kk[i] * kk[i], axis=-1, keepdims=True) + NORM_EPS) for i in us]
            for i in us:
                rows = pl.ds(r0[i], ROW_TILE)
                q_s[rows, :] = qq[i] * qn[i]
                k_s[rows, :] = kk[i] * kn[i]
                v_s[rows, :] = qkv[i][:, 2 * dk:]
                gate_s[h, rows, :] = p[i][:, 3 * dk:]
                beta = jax.nn.sigmoid(bd[i][:, 0:1])
                g = -jnp.exp(hp_ref[h, 0:1, :]) * jax.nn.softplus(bd[i][:, 1:2] + hp_ref[h, 1:2, :])
                beta_s[rows, :] = jnp.broadcast_to(beta, (ROW_TILE, dk))
                g_s[rows, :] = g
            return u[1][ROW_TILE - SUBLANES:, :]

        lax.fori_loop(0, s // (2 * ROW_TILE), proj, jnp.zeros((SUBLANES, 3 * dk), F32))

        def intra(grp, c1):
            js = range(GDN_UNROLL)
            ci = [grp * GDN_UNROLL + j for j in js]
            r0 = [pl.multiple_of(ci[j] * c, c) for j in js]
            q = [q_s[pl.ds(r0[j], c), :] for j in js]
            k = [k_s[pl.ds(r0[j], c), :] for j in js]
            v = [v_s[pl.ds(r0[j], c), :] for j in js]
            beta = [beta_s[pl.ds(r0[j], c), :] for j in js]
            gc = [_split_dot_nt_left(tri, g_s[pl.ds(r0[j], c), :]) for j in js]
            gc_row = [_split_dot_nt(avg, gc[j]) for j in js]
            dec = [jnp.exp(jnp.minimum(gc[j][:, :c] - gc_row[j], 0.0)) for j in js]
            kb = [k[j] * beta[j] for j in js]
            qk = [_bdot_nt(jnp.concatenate([kb[j], q[j]], axis=0), k[j]) for j in js]
            a = [jnp.where(col < row, qk[j][:c] * dec[j], 0.0) for j in js]
            attn = [jnp.where(col <= row, qk[j][c:] * dec[j], 0.0) for j in js]
            tinv = [eye - a[j] for j in js]
            ak = a
            n = 1
            while n < c // 2:
                ak = [_bdot(ak[j], ak[j]) for j in js]
                tinv = [tinv[j] + _bdot(tinv[j], ak[j]) for j in js]
                n *= 2
            egc = [jnp.exp(gc[j]) for j in js]
            uw = [_bdot(tinv[j], jnp.concatenate([v[j] * beta[j], kb[j] * egc[j]], axis=1))
                  for j in js]
            for j in js:
                gl = gc[j][c - 1:c, :]
                u_s[h, pl.ds(r0[j], c), :] = uw[j][:, :dv]
                wq_s[h, ci[j]] = jnp.concatenate([uw[j][:, dv:], q[j] * egc[j]], axis=0).astype(BF16)
                kd_s[h, pl.ds(r0[j], c), :] = (k[j] * jnp.exp(gl - gc[j])).astype(BF16)
                attn_s[h, pl.ds(r0[j], c), :] = attn[j].astype(BF16)
                egl_s[h, pl.ds(ci[j], 1), :] = jnp.exp(gl)
            return c1

        lax.fori_loop(0, nck // GDN_UNROLL, intra, 0)
        return carry

    lax.fori_loop(0, GDN_HEADS, head, 0)

    st_s[...] = jnp.zeros_like(st_s)

    def inter(ci, c2):
        r0 = pl.multiple_of(ci * c, c)
        hs = range(GDN_HEADS)
        tn = (((0,), (0,)), ((), ()))
        st = [st_s[h] for h in hs]
        ws_qs = [jnp.dot(wq_s[h, ci], st[h].astype(BF16), preferred_element_type=F32) for h in hs]
        v_new = [(u_s[h, pl.ds(r0, c), :] - ws_qs[h][:c]).astype(BF16) for h in hs]
        o = [ws_qs[h][c:] + jnp.dot(attn_s[h, pl.ds(r0, c), :], v_new[h], preferred_element_type=F32)
             for h in hs]
        kv = [lax.dot_general(kd_s[h, pl.ds(r0, c), :], v_new[h], tn, preferred_element_type=F32)
              for h in hs]
        for h in hs:
            st_s[h] = st[h] * egl_s[h, pl.ds(ci, 1), :] + kv[h]
            ms = jnp.mean(o[h] * o[h], axis=-1, keepdims=True)
            y = o[h] * lax.rsqrt(ms + NORM_EPS) * nw_ref[...] * _silu(gate_s[h, pl.ds(r0, c), :])
            o_ref[0, pl.ds(r0, c), h * dv:(h + 1) * dv] = y.astype(o_ref.dtype)
        return c2

    lax.fori_loop(0, nck, inter, 0)


def _gdn_call(xb, w, wbd, cw, hp, nw, tri):
    bsz, s, d = xb.shape
    nh = GDN_HEADS
    nck = s // GDN_CHUNK
    const = lambda a: pl.BlockSpec(a.shape, lambda b: (0,) * a.ndim)
    return pl.pallas_call(
        _gdn_kernel,
        grid=(bsz,),
        in_specs=[pl.BlockSpec((1, s, d), lambda b: (b, 0, 0)),
                  const(w), const(wbd), const(cw), const(hp), const(nw), const(tri)],
        out_specs=pl.BlockSpec((1, s, nh * GDN_DV), lambda b: (b, 0, 0)),
        out_shape=jax.ShapeDtypeStruct((bsz, s, nh * GDN_DV), BF16),
        scratch_shapes=[pltpu.VMEM((s, GDN_DK), F32)] * 5
        + [pltpu.VMEM((nh, s, GDN_DV), F32),
           pltpu.VMEM((nh, s, GDN_DV), F32),
           pltpu.VMEM((nh, nck, 2 * GDN_CHUNK, GDN_DK), BF16),
           pltpu.VMEM((nh, s, GDN_DK), BF16),
           pltpu.VMEM((nh, s, GDN_CHUNK), BF16),
           pltpu.VMEM((nh, nck, GDN_DK), F32),
           pltpu.VMEM((nh, GDN_DK, GDN_DV), F32)],
        compiler_params=pltpu.CompilerParams(dimension_semantics=("arbitrary",),
                                             vmem_limit_bytes=VMEM_LIMIT),
        name="mixer_gdn",
    )(xb, w, wbd, cw, hp, nw, tri)


def _merge_kernel(x_ref, xb_ref, ya_ref, yb_ref, yc_ref, yd_ref, wm_ref, wb_ref, wo_ref,
                  g_ref, b_ref, o_ref, ob_ref):
    xb = xb_ref[...]
    merged = None
    for n, y_ref in enumerate((ya_ref, yb_ref, yc_ref, yd_ref)):
        gate = jax.nn.sigmoid(jnp.dot(xb, wm_ref[:, n * D_MODEL:(n + 1) * D_MODEL],
                                      preferred_element_type=F32))
        term = gate * jnp.dot(y_ref[...], wb_ref[n], preferred_element_type=F32)
        merged = term if merged is None else merged + term
    h = _bdot(merged, wo_ref[...])
    y = _layer_norm(DN_ALPHA * x_ref[...] + h, g_ref[...], b_ref[...])
    o_ref[...] = y
    ob_ref[...] = y.astype(BF16)


def _merge_call(x, xb, ya, yb, yc, yd, wm, wb, wo, g, b):
    t, d = x.shape
    tm = MERGE_TILE
    tok = lambda width: pl.BlockSpec((tm, width), lambda i: (i, 0))
    const = lambda shape: pl.BlockSpec(shape, lambda i: (0,) * len(shape),
                                       pipeline_mode=pl.Buffered(1))
    return pl.pallas_call(
        _merge_kernel,
        grid=(t // tm,),
        in_specs=[tok(d), tok(d), tok(BRANCH_WIDTH), tok(BRANCH_WIDTH), tok(BRANCH_WIDTH),
                  tok(BRANCH_WIDTH), const(wm.shape), const(wb.shape), const(wo.shape),
                  const(g.shape), const(b.shape)],
        out_specs=[tok(d), tok(d)],
        out_shape=[jax.ShapeDtypeStruct((t, d), F32), jax.ShapeDtypeStruct((t, d), BF16)],
        compiler_params=pltpu.CompilerParams(dimension_semantics=("arbitrary",),
                                             vmem_limit_bytes=VMEM_LIMIT),
        name="merge",
    )(x, xb, ya, yb, yc, yd, wm, wb, wo, g, b)


def _route(scores, biased):
    s = [scores[e:e + 1, :] for e in range(N_EXPERTS)]
    b = [biased[e:e + 1, :] for e in range(N_EXPERTS)]
    gs = []
    for g in range(N_GROUPS):
        m = b[EXPERTS_PER_GROUP * g:EXPERTS_PER_GROUP * (g + 1)]
        best = None
        for i in range(EXPERTS_PER_GROUP):
            for j in range(i + 1, EXPERTS_PER_GROUP):
                ps = m[i] + m[j]
                best = ps if best is None else jnp.maximum(best, ps)
        gs.append(best)
    gidx = jnp.zeros_like(gs[0], dtype=jnp.int32)
    gval = gs[0]
    for g in range(1, N_GROUPS):
        take = gs[g] > gval
        gidx = jnp.where(take, g, gidx)
        gval = jnp.where(take, gs[g], gval)
    w = []
    for e in range(N_EXPERTS):
        g = e // EXPERTS_PER_GROUP
        beaten = jnp.zeros_like(gidx)
        for j in range(EXPERTS_PER_GROUP * g, EXPERTS_PER_GROUP * (g + 1)):
            if j == e:
                continue
            wins = (b[j] >= b[e]) if j < e else (b[j] > b[e])
            beaten = beaten + wins.astype(jnp.int32)
        sel = (gidx == g) & (beaten < 2)
        w.append(jnp.where(sel, s[e], 0.0))
    tot = w[0]
    for e in range(1, N_EXPERTS):
        tot = tot + w[e]
    return [w[e] / tot for e in range(N_EXPERTS)], gidx


def _moe_kernel(x_ref, xb_ref, wr_ref, rb_ref, su_ref, wgu_ref, wd_ref, g_ref, b_ref,
                o_ref, ob_ref, acc_ref, tr_ref, slotr_ref, slotc_ref, chl_ref, cnt_ref):
    g = pl.program_id(1)
    tm = x_ref.shape[0]
    cc = MOE_CHUNK

    @pl.when(g == 0)
    def _():
        logits = lax.dot_general(wr_ref[...], x_ref[...], (((1,), (1,)), ((), ())),
                                 precision=lax.Precision.HIGHEST, preferred_element_type=F32)
        scores = jax.nn.sigmoid(logits)
        rows, gidx = _route(scores, scores + rb_ref[...])
        member = [(gidx == k).astype(F32) for k in range(N_GROUPS)]
        mm = jnp.concatenate(member + [jnp.zeros((SUBLANES - N_GROUPS, tm), F32)], axis=0)
        rank = jnp.dot(mm.astype(BF16), su_ref[...], preferred_element_type=F32)
        slot = jnp.where(mm > 0.0, rank, -1.0)
        slotr_ref[...] = slot
        for k in range(N_GROUPS):
            cnt_ref[k] = jnp.sum(member[k]).astype(jnp.int32)
        tr_ref[...] = jnp.zeros_like(tr_ref)
        for k in range(N_EXPERTS):
            tr_ref[k:k + 1, :] = rows[k]
        tr_ref[N_EXPERTS:N_EXPERTS + SUBLANES, :] = slot
        tc = tr_ref[...].T
        hi = tc.astype(BF16)
        chl_ref[:, :LANES] = hi
        chl_ref[:, LANES:] = (tc - hi.astype(F32)).astype(BF16)
        for k in range(N_GROUPS):
            slotc_ref[k] = jnp.broadcast_to(tc[:, N_EXPERTS + k:N_EXPERTS + k + 1], (tm, LANES))
        acc_ref[...] = jnp.zeros_like(acc_ref)

    slot_row = slotr_ref[pl.ds(g, 1), :]

    def expert_pass(base, rows):
        slot_col = jnp.concatenate([slotc_ref[g]] * (rows // LANES), axis=1)
        lane = lax.broadcasted_iota(jnp.int32, (rows, LANES), 1)
        row_i = lax.broadcasted_iota(jnp.int32, (rows, tm), 0).astype(F32)
        col_i = lax.broadcasted_iota(jnp.int32, (tm, rows), 1).astype(F32)
        basef = base.astype(F32)
        gather = (slot_row == row_i + basef).astype(BF16)
        xg = jnp.dot(gather, xb_ref[...], preferred_element_type=F32).astype(BF16)
        cw2 = jnp.dot(gather, chl_ref[...], preferred_element_type=F32)
        cw = cw2[:, :LANES] + cw2[:, LANES:]
        yg = None
        for k in range(EXPERTS_PER_GROUP):
            hgu = jnp.dot(xg, wgu_ref[0, k], preferred_element_type=F32)
            h = _silu(hgu[:, :D_EXPERT]) * hgu[:, D_EXPERT:]
            ce = jnp.sum(jnp.where(lane == g * EXPERTS_PER_GROUP + k, cw, 0.0), axis=-1, keepdims=True)
            term = ce * _bdot(h, wd_ref[0, k])
            yg = term if yg is None else yg + term
        scatter = (slot_col == col_i + basef).astype(BF16)
        acc_ref[...] += jnp.dot(scatter, yg.astype(BF16), preferred_element_type=F32)

    half = cc // 2
    n = cnt_ref[g]
    rem = jnp.bitwise_and(n, cc - 1)
    small = jnp.logical_and(rem > 0, rem <= half).astype(jnp.int32)
    n_full = jnp.right_shift(n + cc - 1, MOE_CHUNK_LOG2) - small

    def full_pass(ci, carry):
        expert_pass(ci * cc, cc)
        return carry

    def half_pass(ci, carry):
        expert_pass(n_full * cc, half)
        return carry

    lax.fori_loop(0, n_full, full_pass, 0)
    lax.fori_loop(0, small, half_pass, 0)

    @pl.when(g == N_GROUPS - 1)
    def _():
        out = _layer_norm(DN_ALPHA * x_ref[...] + acc_ref[...], g_ref[...], b_ref[...])
        o_ref[...] = out
        ob_ref[...] = out.astype(BF16)


def _moe_call(x, xb, wr, rb, su, wgu, wd, g, b):
    t, d = x.shape
    tm = MOE_TILE
    epg = EXPERTS_PER_GROUP
    tok = lambda mode=None: pl.BlockSpec((tm, d), lambda i, k: (i, 0), pipeline_mode=mode)
    const = lambda shape, mode=None: pl.BlockSpec(shape, lambda i, k: (0,) * len(shape),
                                                  pipeline_mode=mode)
    return pl.pallas_call(
        _moe_kernel,
        grid=(t // tm, N_GROUPS),
        in_specs=[tok(pl.Buffered(1)), tok(pl.Buffered(1)), const(wr.shape), const(rb.shape),
                  const(su.shape, pl.Buffered(1)),
                  pl.BlockSpec((1, epg, d, 2 * D_EXPERT), lambda i, k: (k, 0, 0, 0)),
                  pl.BlockSpec((1, epg, D_EXPERT, d), lambda i, k: (k, 0, 0, 0)),
                  const(g.shape), const(b.shape)],
        out_specs=[tok(), tok()],
        out_shape=[jax.ShapeDtypeStruct((t, d), F32), jax.ShapeDtypeStruct((t, d), BF16)],
        scratch_shapes=[pltpu.VMEM((tm, d), F32),
                        pltpu.VMEM((LANES, tm), F32),
                        pltpu.VMEM((SUBLANES, tm), F32),
                        pltpu.VMEM((N_GROUPS, tm, LANES), F32),
                        pltpu.VMEM((tm, 2 * LANES), BF16),
                        pltpu.SMEM((N_GROUPS,), jnp.int32)],
        compiler_params=pltpu.CompilerParams(dimension_semantics=("arbitrary", "arbitrary"),
                                             vmem_limit_bytes=VMEM_LIMIT),
        name="moe",
    )(x, xb, wr, rb, su, wgu, wd, g, b)


def _block_diag(w):
    g, n, _ = w.shape
    eye = jnp.eye(g, dtype=w.dtype)
    return (eye[:, None, :, None] * w[:, :, None, :]).reshape(g * n, g * n)


def _per_head(w, off, n_heads, width, parts):
    cols = [w[:, off + p * n_heads * width: off + (p + 1) * n_heads * width]
            .reshape(w.shape[0], n_heads, width) for p in parts]
    return jnp.transpose(jnp.concatenate(cols, axis=2), (1, 0, 2))


def _row(v):
    return v.reshape(1, -1).astype(F32)


def kernel(x, w_in, conv_a_w, conv_a_b, rg_w_a, rg_b_a, rg_w_x, rg_b_x, rg_lambda, gdn_conv_w, gdn_a_log, gdn_dt_bias, gdn_norm_w, gla_w_gate_up, gla_b_gate, gla_norm_w, w_branch, w_out, ln1_g, ln1_b, w_router, router_bias, w_gate, w_up, w_down, ln2_g, ln2_b):
    bsz, s, d = x.shape
    t = bsz * s
    c = GDN_CHUNK
    ii = jnp.arange(SB_BLOCK)
    u2 = jnp.concatenate([(ii[:, None] >= ii[None, :]).astype(BF16),
                          jnp.ones((SB_BLOCK, SB_BLOCK), BF16)], axis=1)
    u2 = jnp.concatenate([u2, u2], axis=0)
    jj = jnp.arange(c)
    tri = (jj[:, None] >= jj[None, :]).astype(BF16)
    wr = w_router.astype(F32).T
    rb = router_bias.astype(F32).reshape(N_EXPERTS, 1)
    kk = jnp.arange(MOE_TILE)
    su = (kk[:, None] < kk[None, :]).astype(BF16)

    xf = x.reshape(t, d)
    xb = xf.astype(BF16)
    for l in range(DEPTH):
        wl = w_in[l]
        wlb = wl.astype(BF16)
        xb3 = xb.reshape(bsz, s, d)

        wg = jnp.concatenate([_block_diag(rg_w_a[l]), _block_diag(rg_w_x[l])], axis=1).astype(BF16)
        bg = jnp.concatenate([rg_b_a[l], rg_b_x[l]]).reshape(1, -1).astype(F32)
        ya = _lru_call(xb3, wlb[:, _OFF_A:_OFF_A + LRU_WIDTH], conv_a_w[l].astype(F32),
                       _row(conv_a_b[l]), wg, bg, _row(rg_lambda[l]))

        w_gdn = jnp.concatenate([_per_head(wlb, _OFF_GDN_QKV, GDN_HEADS, GDN_DK, (0, 1, 2)),
                                 _per_head(wlb, _OFF_GDN_GATE, GDN_HEADS, GDN_DV, (0,))], axis=2)
        wbd = jnp.stack([wlb[:, _OFF_GDN_BETA:_OFF_GDN_BETA + GDN_HEADS],
                         wlb[:, _OFF_GDN_DECAY:_OFF_GDN_DECAY + GDN_HEADS]], axis=2)
        wbd = jnp.pad(jnp.transpose(wbd, (1, 0, 2)), ((0, 0), (0, 0), (0, LANES - 2)))
        cw_gdn = jnp.transpose(gdn_conv_w[l].astype(F32).reshape(CONV_WIDTH, 3, GDN_HEADS, GDN_DK),
                               (2, 0, 1, 3)).reshape(GDN_HEADS, CONV_WIDTH, 3 * GDN_DK)
        hp = jnp.zeros((GDN_HEADS, SUBLANES, GDN_DK), F32)
        hp = hp.at[:, 0, :].set(gdn_a_log[l].astype(F32)[:, None])
        hp = hp.at[:, 1, :].set(gdn_dt_bias[l].astype(F32)[:, None])
        yb = _gdn_call(xb3, w_gdn, wbd, cw_gdn, hp, _row(gdn_norm_w[l]), tri)

        yc = _sb_call(xb3, wlb[:, _OFF_SB:_OFF_SB + 3 * SB_HEADS * SB_DH], u2)

        q_off = _OFF_GLA_QKV
        w_gla = jnp.concatenate([
            _per_head(wlb, q_off, GLA_HEADS, GLA_DK, (0, 1)),
            _per_head(wlb, q_off + 2 * GLA_HEADS * GLA_DK, GLA_HEADS, GLA_DV, (0,)),
            _per_head(wlb, _OFF_GLA_GATE, GLA_HEADS, GLA_DV, (0,))], axis=2)
        wlr = jnp.pad(wlb[:, _OFF_GLA_LR:_OFF_GLA_LR + GLA_GATE_RANK],
                      ((0, 0), (0, LANES - GLA_GATE_RANK)))
        wgu = jnp.pad(jnp.transpose(gla_w_gate_up[l].reshape(GLA_GATE_RANK, GLA_HEADS, GLA_DK), (1, 0, 2)),
                      ((0, 0), (0, LANES - GLA_GATE_RANK), (0, 0))).astype(BF16)
        bgl = gla_b_gate[l].astype(F32).reshape(GLA_HEADS, 1, GLA_DK)
        yd = _gla_call(xb3, w_gla, wlr, wgu, bgl, _row(gla_norm_w[l]), tri)

        xf, xb = _merge_call(xf, xb, ya.reshape(t, -1), yb.reshape(t, -1), yc.reshape(t, -1),
                             yd.reshape(t, -1), wlb[:, _OFF_MERGE:], w_branch[l].astype(BF16),
                             w_out[l].astype(BF16), _row(ln1_g[l]), _row(ln1_b[l]))

        w_gu = jnp.concatenate([w_gate[l].astype(BF16), w_up[l].astype(BF16)], axis=2)
        w_gu = w_gu.reshape(N_GROUPS, EXPERTS_PER_GROUP, d, 2 * D_EXPERT)
        w_dn = w_down[l].astype(BF16).reshape(N_GROUPS, EXPERTS_PER_GROUP, D_EXPERT, d)
        xf, xb = _moe_call(xf, xb, wr, rb, su, w_gu, w_dn, _row(ln2_g[l]), _row(ln2_b[l]))
    return xf.reshape(bsz, s, d)
```

```python
import functools

import jax
import jax.numpy as jnp
from jax import lax
from jax.experimental import pallas as pl
from jax.experimental.pallas import tpu as pltpu

F32 = jnp.float32
BF16 = jnp.bfloat16

D_MODEL = 1024
DEPTH = 4
LRU_WIDTH = 512
LRU_BLOCKS = 8
LRU_C = 8.0
CONV_WIDTH = 4
GDN_HEADS = 4
GDN_DK = 128
GDN_DV = 128
GDN_CHUNK = 64
SB_HEADS = 8
SB_DH = 64
SB_BLOCK = 128
GLA_HEADS = 4
GLA_DK = 128
GLA_DV = 128
GLA_GATE_RANK = 16
GLA_TAU = 16.0
GLA_CHUNK = 64
N_BRANCHES = 4
BRANCH_WIDTH = 512
N_EXPERTS = 16
N_GROUPS = 4
EXPERTS_PER_GROUP = 4
D_EXPERT = 512
LN_EPS = 1e-5
NORM_EPS = 1e-6
DN_ALPHA = (2 * DEPTH) ** 0.25
LOG2E = 1.4426950408889634
SB_UNDERFLOW_LOG2 = 150.0

_OFF_A = 0
_OFF_GDN_QKV = _OFF_A + LRU_WIDTH
_OFF_GDN_BETA = _OFF_GDN_QKV + 3 * GDN_HEADS * GDN_DK
_OFF_GDN_DECAY = _OFF_GDN_BETA + GDN_HEADS
_OFF_GDN_GATE = _OFF_GDN_DECAY + GDN_HEADS
_OFF_SB = _OFF_GDN_GATE + GDN_HEADS * GDN_DV
_OFF_GLA_QKV = _OFF_SB + 3 * SB_HEADS * SB_DH
_OFF_GLA_LR = _OFF_GLA_QKV + GLA_HEADS * (2 * GLA_DK + GLA_DV)
_OFF_GLA_GATE = _OFF_GLA_LR + GLA_GATE_RANK
_OFF_MERGE = _OFF_GLA_GATE + GLA_HEADS * GLA_DV

LANES = 128
SUBLANES = 8
ROW_TILE = 256
LRU_SUBTILES = 2
MERGE_TILE = 512
CHUNK_UNROLL = 8
GDN_UNROLL = 16
MOE_TILE = 1024
MOE_CHUNK_LOG2 = 8
MOE_CHUNK = 1 << MOE_CHUNK_LOG2
VMEM_LIMIT = 56 * 1024 * 1024


def _bdot(a, b):
    return jnp.dot(a.astype(BF16), b.astype(BF16), preferred_element_type=F32)


def _bdot_nt(a, b):
    return lax.dot_general(a.astype(BF16), b.astype(BF16), (((1,), (1,)), ((), ())),
                           preferred_element_type=F32)


def _bdot_tn(a, b):
    return lax.dot_general(a.astype(BF16), b.astype(BF16), (((0,), (0,)), ((), ())),
                           preferred_element_type=F32)


def _split_dot(a, b_bf16):
    hi = a.astype(BF16)
    lo = (a - hi.astype(F32)).astype(BF16)
    return (jnp.dot(hi, b_bf16, preferred_element_type=F32)
            + jnp.dot(lo, b_bf16, preferred_element_type=F32))


def _split_dot_nt(a_bf16, b):
    hi = b.astype(BF16)
    lo = (b - hi.astype(F32)).astype(BF16)
    dn = (((1,), (1,)), ((), ()))
    return (lax.dot_general(a_bf16, hi, dn, preferred_element_type=F32)
            + lax.dot_general(a_bf16, lo, dn, preferred_element_type=F32))


def _silu(x):
    return x * jax.nn.sigmoid(x)


def _layer_norm(x, g, b):
    mu = jnp.mean(x, axis=-1, keepdims=True)
    xc = x - mu
    var = jnp.mean(xc * xc, axis=-1, keepdims=True)
    return xc * lax.rsqrt(var + LN_EPS) * g + b


def _shift_rows(u, tail, s, row8):
    ur = pltpu.roll(u, s, 0)
    top = jnp.where(row8 >= s, ur[:SUBLANES], pltpu.roll(tail, s, 0))
    return jnp.concatenate([top, ur[SUBLANES:]], axis=0)


def _causal_conv(u, tail, cw_ref, row8):
    y = u * cw_ref[CONV_WIDTH - 1:CONV_WIDTH, :]
    for s in range(1, CONV_WIDTH):
        k = CONV_WIDTH - 1 - s
        y = y + _shift_rows(u, tail, s, row8) * cw_ref[k:k + 1, :]
    return y


def _lru_kernel(xb_ref, w_ref, cw_ref, cb_ref, wg_ref, bg_ref, lam_ref, o_ref, tail_ref, h_ref):
    t = pl.program_id(1)
    tt = ROW_TILE
    nsub = xb_ref.shape[1] // tt
    subs = range(nsub)
    w = LRU_WIDTH

    @pl.when(t == 0)
    def _():
        tail_ref[...] = jnp.zeros_like(tail_ref)
        h_ref[...] = jnp.zeros_like(h_ref)

    u = [jnp.dot(xb_ref[0, i * tt:(i + 1) * tt, :], w_ref[...], preferred_element_type=F32)
         for i in subs]
    row = lax.broadcasted_iota(jnp.int32, (tt, w), 0)
    row8 = lax.broadcasted_iota(jnp.int32, (SUBLANES, w), 0)
    tails = [tail_ref[...]] + [u[i][tt - SUBLANES:, :] for i in subs[:-1]]
    xc = [_causal_conv(u[i], tails[i], cw_ref, row8) + cb_ref[...] for i in subs]
    tail_ref[...] = u[nsub - 1][tt - SUBLANES:, :]

    gates = [_bdot(xc[i], wg_ref[...]) + bg_ref[...] for i in subs]
    sp = jax.nn.softplus(-lam_ref[...])
    log_a = [(-LRU_C) * jax.nn.sigmoid(gates[i][:, :w]) * sp for i in subs]
    a = [jnp.exp(log_a[i]) for i in subs]
    mult = [jnp.sqrt(-jnp.tanh(log_a[i]) * (a[i] * a[i] + 1.0)) for i in subs]
    mult[0] = jnp.where(row + t == 0, 1.0, mult[0])
    b = [mult[i] * jax.nn.sigmoid(gates[i][:, w:]) * xc[i] for i in subs]

    d = 1
    while d < SUBLANES:
        keep = jnp.bitwise_and(row, SUBLANES - 1) >= d
        a_sh = [jnp.where(keep, pltpu.roll(a[i], d, 0), 1.0) for i in subs]
        b_sh = [jnp.where(keep, pltpu.roll(b[i], d, 0), 0.0) for i in subs]
        b = [a[i] * b_sh[i] + b[i] for i in subs]
        a = [a[i] * a_sh[i] for i in subs]
        d *= 2
    carry = h_ref[...]
    for i in subs:
        hs = []
        for gi in range(tt // SUBLANES):
            rows = slice(gi * SUBLANES, (gi + 1) * SUBLANES)
            hg = a[i][rows, :] * carry + b[i][rows, :]
            carry = jnp.broadcast_to(hg[SUBLANES - 1:SUBLANES, :], (SUBLANES, w))
            hs.append(hg)
        o_ref[0, i * tt:(i + 1) * tt, :] = jnp.concatenate(hs, axis=0).astype(o_ref.dtype)
    h_ref[...] = carry


def _lru_call(xb, w, cw, cb, wg, bg, lam):
    bsz, s, d = xb.shape
    tt = LRU_SUBTILES * ROW_TILE
    const = lambda shape: pl.BlockSpec(shape, lambda b, t: (0,) * len(shape))
    return pl.pallas_call(
        _lru_kernel,
        grid=(bsz, s // tt),
        in_specs=[pl.BlockSpec((1, tt, d), lambda b, t: (b, t, 0)),
                  const(w.shape), const(cw.shape), const(cb.shape), const(wg.shape),
                  const(bg.shape), const(lam.shape)],
        out_specs=pl.BlockSpec((1, tt, LRU_WIDTH), lambda b, t: (b, t, 0)),
        out_shape=jax.ShapeDtypeStruct((bsz, s, LRU_WIDTH), BF16),
        scratch_shapes=[pltpu.VMEM((SUBLANES, LRU_WIDTH), F32), pltpu.VMEM((SUBLANES, LRU_WIDTH), F32)],
        compiler_params=pltpu.CompilerParams(dimension_semantics=("arbitrary", "arbitrary"),
                                             vmem_limit_bytes=VMEM_LIMIT),
        name="mixer_lru",
    )(xb, w, cw, cb, wg, bg, lam)


def _sb_kernel(xb_ref, w_ref, u2_ref, hsel_ref, o_ref, q_s, k_s, v_s, acc_s, run_s):
    s = xb_ref.shape[1]
    blk = SB_BLOCK
    dh = SB_DH
    nh = SB_HEADS
    hd = nh * dh

    def colmax(x):
        m = x[:SUBLANES]
        for gi in range(1, ROW_TILE // SUBLANES):
            m = jnp.maximum(m, x[gi * SUBLANES:(gi + 1) * SUBLANES])
        return m

    def proj(t, carry):
        cq, ck = carry
        r0 = pl.multiple_of(t * ROW_TILE, ROW_TILE)
        qkv = jnp.dot(xb_ref[0, pl.ds(r0, ROW_TILE), :], w_ref[...], preferred_element_type=F32)
        qb = (qkv[:, :hd] * (dh ** -0.5 * LOG2E)).astype(BF16)
        kb = qkv[:, hd:2 * hd].astype(BF16)
        for h in range(nh):
            q_s[h, pl.ds(r0, ROW_TILE), :] = qb[:, h * dh:(h + 1) * dh]
            k_s[h, pl.ds(r0, ROW_TILE), :] = kb[:, h * dh:(h + 1) * dh]
            v_s[h, pl.ds(r0, ROW_TILE), :] = qkv[:, 2 * hd + h * dh:2 * hd + (h + 1) * dh].astype(BF16)
        qf = qb.astype(F32)
        kf = kb.astype(F32)
        nq = _bdot(qf * qf, hsel_ref[...])
        nk = _bdot(kf * kf, hsel_ref[...])
        return jnp.maximum(cq, colmax(nq)), jnp.maximum(ck, colmax(nk))

    zeros8 = jnp.zeros((SUBLANES, LANES), F32)
    cq, ck = lax.fori_loop(0, s // ROW_TILE, proj, (zeros8, zeros8))

    zb2 = jnp.max(cq, axis=0, keepdims=True) * jnp.max(ck, axis=0, keepdims=True)
    stop_below = -(SB_UNDERFLOW_LOG2 + 1.02 * jnp.max(jnp.sqrt(zb2)))

    row = lax.broadcasted_iota(jnp.int32, (blk, blk), 0)
    col = lax.broadcasted_iota(jnp.int32, (blk, blk), 1)
    u2 = u2_ref[...]

    def max_run():
        m = run_s[0]
        for h in range(1, nh):
            m = jnp.maximum(m, run_s[h])
        return jnp.max(m)

    def qloop(qi, c):
        q0 = pl.multiple_of(qi * blk, blk)
        acc_s[...] = jnp.zeros_like(acc_s)
        run_s[...] = jnp.zeros_like(run_s)

        def blocks(k0s, diagonal):
            ch = [(h, j) for j in range(len(k0s)) for h in range(nh)]
            nt = (((1,), (1,)), ((), ()))
            z = {c_: lax.dot_general(q_s[c_[0], pl.ds(q0, blk), :], k_s[c_[0], pl.ds(k0s[c_[1]], blk), :],
                                     nt, preferred_element_type=F32) for c_ in ch}
            nz = {c_: -z[c_] for c_ in ch}
            lk = {c_: jnp.minimum(nz[c_], 0.0) - jnp.log2(1.0 + jnp.exp2(jnp.minimum(z[c_], nz[c_])))
                  for c_ in ch}
            lk = {c_: jnp.where(col < row, lk[c_], 0.0) if diagonal[c_[1]] else lk[c_] for c_ in ch}
            hi = {c_: lk[c_].astype(BF16) for c_ in ch}
            lo = {c_: (lk[c_] - hi[c_].astype(F32)).astype(BF16) for c_ in ch}
            cs2 = {c_: jnp.dot(jnp.concatenate([hi[c_], lo[c_]], axis=1), u2, preferred_element_type=F32)
                   for c_ in ch}
            run = {}
            for h in range(nh):
                r = run_s[h]
                for j in range(len(k0s)):
                    run[(h, j)] = r
                    r = r + cs2[(h, j)][:, blk:]
                run_s[h] = r
            p = {c_: jnp.exp2(z[c_] + cs2[c_][:, :blk] + run[c_]) for c_ in ch}
            p = {c_: jnp.where(col < row, p[c_], 0.0) if diagonal[c_[1]] else p[c_] for c_ in ch}
            pv = {c_: jnp.dot(p[c_].astype(BF16), v_s[c_[0], pl.ds(k0s[c_[1]], blk), :],
                              preferred_element_type=F32) for c_ in ch}
            for h in range(nh):
                tot = pv[(h, 0)]
                for j in range(1, len(k0s)):
                    tot = tot + pv[(h, j)]
                acc_s[h] += tot

        blocks([q0], (True,))
        npairs = jnp.right_shift(qi, 1)

        def more(c2):
            return jnp.logical_and(c2[0] < npairs, c2[1] > stop_below)

        def pair(c2):
            i = c2[0]
            ka = pl.multiple_of((qi - 1 - 2 * i) * blk, blk)
            kb = pl.multiple_of((qi - 2 - 2 * i) * blk, blk)
            blocks([ka, kb], (False, False))
            return i + 1, max_run()

        _, top = lax.while_loop(more, pair, (jnp.int32(0), max_run()))

        def last(i, c2):
            blocks([0], (False,))
            return c2

        live = jnp.logical_and(jnp.bitwise_and(qi, 1) == 1, top > stop_below)
        lax.fori_loop(0, live.astype(jnp.int32), last, 0)
        for h in range(nh):
            o_ref[0, pl.ds(q0, blk), h * dh:(h + 1) * dh] = acc_s[h].astype(o_ref.dtype)
        return c

    lax.fori_loop(0, s // blk, qloop, 0)


def _sb_call(xb, w, u2):
    bsz, s, d = xb.shape
    hsel = (jnp.arange(SB_HEADS * SB_DH)[:, None] // SB_DH == jnp.arange(LANES)[None, :]).astype(BF16)
    return pl.pallas_call(
        _sb_kernel,
        grid=(bsz,),
        in_specs=[pl.BlockSpec((1, s, d), lambda b: (b, 0, 0)),
                  pl.BlockSpec(w.shape, lambda b: (0, 0)),
                  pl.BlockSpec(u2.shape, lambda b: (0, 0)),
                  pl.BlockSpec(hsel.shape, lambda b: (0, 0))],
        out_specs=pl.BlockSpec((1, s, SB_HEADS * SB_DH), lambda b: (b, 0, 0)),
        out_shape=jax.ShapeDtypeStruct((bsz, s, SB_HEADS * SB_DH), BF16),
        scratch_shapes=[pltpu.VMEM((SB_HEADS, s, SB_DH), BF16)] * 3
        + [pltpu.VMEM((SB_HEADS, SB_BLOCK, SB_DH), F32), pltpu.VMEM((SB_HEADS, SB_BLOCK, SB_BLOCK), F32)],
        compiler_params=pltpu.CompilerParams(dimension_semantics=("arbitrary",),
                                             vmem_limit_bytes=VMEM_LIMIT),
        name="mixer_sb",
    )(xb, w, u2, hsel)


def _gla_kernel(xb_ref, w_ref, wlr_ref, wgu_ref, bg_ref, nw_ref, tri_ref, o_ref,
                q_s, k_s, v_s, g_s, gate_s, oi_s, qd_s, m_s, eb_s):
    s = xb_ref.shape[1]
    c = GLA_CHUNK
    dk = GLA_DK
    nck = s // c

    def proj(t, carry):
        us = range(2)
        r0 = [pl.multiple_of((2 * t + i) * ROW_TILE, ROW_TILE) for i in us]
        x = [xb_ref[0, pl.ds(r0[i], ROW_TILE), :] for i in us]
        lr = [jnp.dot(x[i], wlr_ref[...], preferred_element_type=F32) for i in us]
        p = [jnp.dot(x[i], w_ref[0], preferred_element_type=F32) for i in us]
        gpre = [_bdot(lr[i], wgu_ref[0]) + bg_ref[0] for i in us]
        for i in us:
            rows = pl.ds(r0[i], ROW_TILE)
            g_s[rows, :] = jax.nn.log_sigmoid(gpre[i]) * (1.0 / GLA_TAU)
            q_s[rows, :] = p[i][:, :dk] * dk ** -0.5
            k_s[rows, :] = p[i][:, dk:2 * dk]
            v_s[rows, :] = p[i][:, 2 * dk:3 * dk]
            gate_s[rows, :] = p[i][:, 3 * dk:]
        return carry

    lax.fori_loop(0, s // (2 * ROW_TILE), proj, 0)

    row = lax.broadcasted_iota(jnp.int32, (c, c), 0)
    col = lax.broadcasted_iota(jnp.int32, (c, c), 1)
    tri = tri_ref[...]

    def intra(grp, carry):
        u = range(CHUNK_UNROLL)
        ci = [grp * CHUNK_UNROLL + j for j in u]
        r0 = [pl.multiple_of(ci[j] * c, c) for j in u]
        q = [q_s[pl.ds(r0[j], c), :] for j in u]
        k = [k_s[pl.ds(r0[j], c), :] for j in u]
        v = [v_s[pl.ds(r0[j], c), :] for j in u]
        bc = [_split_dot_nt_left(tri, g_s[pl.ds(r0[j], c), :]) for j in u]
        blast = [bc[j][c - 1:c, :] for j in u]
        qd = [q[j] * jnp.exp(bc[j]) for j in u]
        ki = [k[j] * jnp.exp(-bc[j]) for j in u]
        kd = [k[j] * jnp.exp(blast[j] - bc[j]) for j in u]
        attn = [jnp.where(col <= row, _bdot_nt(qd[j], ki[j]), 0.0) for j in u]
        oi = [_bdot(attn[j], v[j]) for j in u]
        m = [_bdot_tn(v[j], kd[j]) for j in u]
        for j in u:
            oi_s[pl.ds(r0[j], c), :] = oi[j]
            qd_s[pl.ds(r0[j], c), :] = qd[j].astype(BF16)
            m_s[ci[j]] = m[j]
            eb_s[pl.ds(ci[j], 1), :] = jnp.exp(blast[j])
        return carry

    lax.fori_loop(0, nck // CHUNK_UNROLL, intra, 0)

    def inter(grp, st):
        for j in range(CHUNK_UNROLL):
            ci = grp * CHUNK_UNROLL + j
            r0 = pl.multiple_of(ci * c, c)
            o = oi_s[pl.ds(r0, c), :] + lax.dot_general(
                qd_s[pl.ds(r0, c), :], st.astype(BF16), (((1,), (1,)), ((), ())),
                preferred_element_type=F32)
            st = st * eb_s[pl.ds(ci, 1), :] + m_s[ci]
            ms = jnp.mean(o * o, axis=-1, keepdims=True)
            y = o * lax.rsqrt(ms + NORM_EPS) * nw_ref[...] * _silu(gate_s[pl.ds(r0, c), :])
            o_ref[0, pl.ds(r0, c), :] = y.astype(o_ref.dtype)
        return st

    lax.fori_loop(0, nck // CHUNK_UNROLL, inter, jnp.zeros((GLA_DV, dk), F32))


def _split_dot_nt_left(tri_bf16, x):
    hi = x.astype(BF16)
    lo = (x - hi.astype(F32)).astype(BF16)
    return (jnp.dot(tri_bf16, hi, preferred_element_type=F32)
            + jnp.dot(tri_bf16, lo, preferred_element_type=F32))


def _gla_call(xb, w, wlr, wgu, bg, nw, tri):
    bsz, s, d = xb.shape
    return pl.pallas_call(
        _gla_kernel,
        grid=(bsz, GLA_HEADS),
        in_specs=[pl.BlockSpec((1, s, d), lambda b, h: (b, 0, 0)),
                  pl.BlockSpec((1, d, 4 * GLA_DK), lambda b, h: (h, 0, 0)),
                  pl.BlockSpec(wlr.shape, lambda b, h: (0, 0)),
                  pl.BlockSpec((1, LANES, GLA_DK), lambda b, h: (h, 0, 0)),
                  pl.BlockSpec((1, 1, GLA_DK), lambda b, h: (h, 0, 0)),
                  pl.BlockSpec(nw.shape, lambda b, h: (0, 0)),
                  pl.BlockSpec(tri.shape, lambda b, h: (0, 0))],
        out_specs=pl.BlockSpec((1, s, GLA_DV), lambda b, h: (b, 0, h)),
        out_shape=jax.ShapeDtypeStruct((bsz, s, GLA_HEADS * GLA_DV), BF16),
        scratch_shapes=[pltpu.VMEM((s, GLA_DK), F32)] * 6
        + [pltpu.VMEM((s, GLA_DK), BF16), pltpu.VMEM((s // GLA_CHUNK, GLA_DV, GLA_DK), F32),
           pltpu.VMEM((s // GLA_CHUNK, GLA_DK), F32)],
        compiler_params=pltpu.CompilerParams(dimension_semantics=("arbitrary", "arbitrary"),
                                             vmem_limit_bytes=VMEM_LIMIT),
        name="mixer_gla",
    )(xb, w, wlr, wgu, bg, nw, tri)


def _gdn_kernel(xb_ref, w_ref, wbd_ref, cw_ref, hp_ref, nw_ref, tri_ref, o_ref,
                q_s, k_s, v_s, beta_s, g_s, gate_s, u_s, wq_s, kd_s, attn_s, egl_s, st_s):
    s = xb_ref.shape[1]
    c = GDN_CHUNK
    dk = GDN_DK
    dv = GDN_DV
    nck = s // c
    row8 = lax.broadcasted_iota(jnp.int32, (SUBLANES, 3 * dk), 0)
    row = lax.broadcasted_iota(jnp.int32, (c, c), 0)
    col = lax.broadcasted_iota(jnp.int32, (c, c), 1)
    eye = (row == col).astype(F32)
    tri = tri_ref[...]
    avg = jnp.full((c, dk), 1.0 / dk, BF16)

    def head(h, carry):
        def proj(t, tail):
            us = range(2)
            r0 = [pl.multiple_of((2 * t + i) * ROW_TILE, ROW_TILE) for i in us]
            x = [xb_ref[0, pl.ds(r0[i], ROW_TILE), :] for i in us]
            p = [jnp.dot(x[i], w_ref[h], preferred_element_type=F32) for i in us]
            bd = [jnp.dot(x[i], wbd_ref[h], preferred_element_type=F32) for i in us]
            u = [p[i][:, :3 * dk] for i in us]
            tails = [tail, u[0][ROW_TILE - SUBLANES:, :]]
            qkv = [_silu(_causal_conv(u[i], tails[i], cw_ref.at[h], row8)) for i in us]
            qq = [qkv[i][:, :dk] for i in us]
            kk = [qkv[i][:, dk:2 * dk] for i in us]
            qn = [lax.rsqrt(jnp.sum(qq[i] * qq[i], axis=-1, keepdims=True) + NORM_EPS) * dk ** -0.5 for i in us]
            kn = [lax.rsqrt(jnp.sum(kk[i] * kk[i], axis=-1, keepdims=True) + NORM_EPS) for i in us]
            for i in us:
                rows = pl.ds(r0[i], ROW_TILE)
                q_s[rows, :] = qq[i] * qn[i]
                k_s[rows, :] = kk[i] * kn[i]
                v_s[rows, :] = qkv[i][:, 2 * dk:]
                gate_s[h, rows, :] = p[i][:, 3 * dk:]
                beta = jax.nn.sigmoid(bd[i][:, 0:1])
                g = -jnp.exp(hp_ref[h, 0:1, :]) * jax.nn.softplus(bd[i][:, 1:2] + hp_ref[h, 1:2, :])
                beta_s[rows, :] = jnp.broadcast_to(beta, (ROW_TILE, dk))
                g_s[rows, :] = g
            return u[1][ROW_TILE - SUBLANES:, :]

        lax.fori_loop(0, s // (2 * ROW_TILE), proj, jnp.zeros((SUBLANES, 3 * dk), F32))

        def intra(grp, c1):
            js = range(GDN_UNROLL)
            ci = [grp * GDN_UNROLL + j for j in js]
            r0 = [pl.multiple_of(ci[j] * c, c) for j in js]
            q = [q_s[pl.ds(r0[j], c), :] for j in js]
            k = [k_s[pl.ds(r0[j], c), :] for j in js]
            v = [v_s[pl.ds(r0[j], c), :] for j in js]
            beta = [beta_s[pl.ds(r0[j], c), :] for j in js]
            gc = [_split_dot_nt_left(tri, g_s[pl.ds(r0[j], c), :]) for j in js]
            gc_row = [_split_dot_nt(avg, gc[j]) for j in js]
            dec = [jnp.exp(jnp.minimum(gc[j][:, :c] - gc_row[j], 0.0)) for j in js]
            kb = [k[j] * beta[j] for j in js]
            qk = [_bdot_nt(jnp.concatenate([kb[j], q[j]], axis=0), k[j]) for j in js]
            a = [jnp.where(col < row, qk[j][:c] * dec[j], 0.0) for j in js]
            attn = [jnp.where(col <= row, qk[j][c:] * dec[j], 0.0) for j in js]
            tinv = [eye - a[j] for j in js]
            ak = a
            n = 1
            while n < c // 2:
                ak = [_bdot(ak[j], ak[j]) for j in js]
                tinv = [tinv[j] + _bdot(tinv[j], ak[j]) for j in js]
                n *= 2
            egc = [jnp.exp(gc[j]) for j in js]
            uw = [_bdot(tinv[j], jnp.concatenate([v[j] * beta[j], kb[j] * egc[j]], axis=1))
                  for j in js]
            for j in js:
                gl = gc[j][c - 1:c, :]
                u_s[h, pl.ds(r0[j], c), :] = uw[j][:, :dv]
                wq_s[h, ci[j]] = jnp.concatenate([uw[j][:, dv:], q[j] * egc[j]], axis=0).astype(BF16)
                kd_s[h, pl.ds(r0[j], c), :] = (k[j] * jnp.exp(gl - gc[j])).astype(BF16)
                attn_s[h, pl.ds(r0[j], c), :] = attn[j].astype(BF16)
                egl_s[h, pl.ds(ci[j], 1), :] = jnp.exp(gl)
            return c1

        lax.fori_loop(0, nck // GDN_UNROLL, intra, 0)
        return carry

    lax.fori_loop(0, GDN_HEADS, head, 0)

    st_s[...] = jnp.zeros_like(st_s)

    def inter(ci, c2):
        r0 = pl.multiple_of(ci * c, c)
        hs = range(GDN_HEADS)
        tn = (((0,), (0,)), ((), ()))
        st = [st_s[h] for h in hs]
        ws_qs = [jnp.dot(wq_s[h, ci], st[h].astype(BF16), preferred_element_type=F32) for h in hs]
        v_new = [(u_s[h, pl.ds(r0, c), :] - ws_qs[h][:c]).astype(BF16) for h in hs]
        o = [ws_qs[h][c:] + jnp.dot(attn_s[h, pl.ds(r0, c), :], v_new[h], preferred_element_type=F32)
             for h in hs]
        kv = [lax.dot_general(kd_s[h, pl.ds(r0, c), :], v_new[h], tn, preferred_element_type=F32)
              for h in hs]
        for h in hs:
            st_s[h] = st[h] * egl_s[h, pl.ds(ci, 1), :] + kv[h]
            ms = jnp.mean(o[h] * o[h], axis=-1, keepdims=True)
            y = o[h] * lax.rsqrt(ms + NORM_EPS) * nw_ref[...] * _silu(gate_s[h, pl.ds(r0, c), :])
            o_ref[0, pl.ds(r0, c), h * dv:(h + 1) * dv] = y.astype(o_ref.dtype)
        return c2

    lax.fori_loop(0, nck, inter, 0)


def _gdn_call(xb, w, wbd, cw, hp, nw, tri):
    bsz, s, d = xb.shape
    nh = GDN_HEADS
    nck = s // GDN_CHUNK
    const = lambda a: pl.BlockSpec(a.shape, lambda b: (0,) * a.ndim)
    return pl.pallas_call(
        _gdn_kernel,
        grid=(bsz,),
        in_specs=[pl.BlockSpec((1, s, d), lambda b: (b, 0, 0)),
                  const(w), const(wbd), const(cw), const(hp), const(nw), const(tri)],
        out_specs=pl.BlockSpec((1, s, nh * GDN_DV), lambda b: (b, 0, 0)),
        out_shape=jax.ShapeDtypeStruct((bsz, s, nh * GDN_DV), BF16),
        scratch_shapes=[pltpu.VMEM((s, GDN_DK), F32)] * 5
        + [pltpu.VMEM((nh, s, GDN_DV), F32),
           pltpu.VMEM((nh, s, GDN_DV), F32),
           pltpu.VMEM((nh, nck, 2 * GDN_CHUNK, GDN_DK), BF16),
           pltpu.VMEM((nh, s, GDN_DK), BF16),
           pltpu.VMEM((nh, s, GDN_CHUNK), BF16),
           pltpu.VMEM((nh, nck, GDN_DK), F32),
           pltpu.VMEM((nh, GDN_DK, GDN_DV), F32)],
        compiler_params=pltpu.CompilerParams(dimension_semantics=("arbitrary",),
                                             vmem_limit_bytes=VMEM_LIMIT),
        name="mixer_gdn",
    )(xb, w, wbd, cw, hp, nw, tri)


def _merge_kernel(x_ref, xb_ref, ya_ref, yb_ref, yc_ref, yd_ref, wm_ref, wb_ref, wo_ref,
                  g_ref, b_ref, o_ref, ob_ref):
    xb = xb_ref[...]
    merged = None
    for n, y_ref in enumerate((ya_ref, yb_ref, yc_ref, yd_ref)):
        gate = jax.nn.sigmoid(jnp.dot(xb, wm_ref[:, n * D_MODEL:(n + 1) * D_MODEL],
                                      preferred_element_type=F32))
        term = gate * jnp.dot(y_ref[...], wb_ref[n], preferred_element_type=F32)
        merged = term if merged is None else merged + term
    h = _bdot(merged, wo_ref[...])
    y = _layer_norm(DN_ALPHA * x_ref[...] + h, g_ref[...], b_ref[...])
    o_ref[...] = y
    ob_ref[...] = y.astype(BF16)


def _merge_call(x, xb, ya, yb, yc, yd, wm, wb, wo, g, b):
    t, d = x.shape
    tm = MERGE_TILE
    tok = lambda width: pl.BlockSpec((tm, width), lambda i: (i, 0))
    const = lambda shape: pl.BlockSpec(shape, lambda i: (0,) * len(shape),
                                       pipeline_mode=pl.Buffered(1))
    return pl.pallas_call(
        _merge_kernel,
        grid=(t // tm,),
        in_specs=[tok(d), tok(d), tok(BRANCH_WIDTH), tok(BRANCH_WIDTH), tok(BRANCH_WIDTH),
                  tok(BRANCH_WIDTH), const(wm.shape), const(wb.shape), const(wo.shape),
                  const(g.shape), const(b.shape)],
        out_specs=[tok(d), tok(d)],
        out_shape=[jax.ShapeDtypeStruct((t, d), F32), jax.ShapeDtypeStruct((t, d), BF16)],
        compiler_params=pltpu.CompilerParams(dimension_semantics=("arbitrary",),
                                             vmem_limit_bytes=VMEM_LIMIT),
        name="merge",
    )(x, xb, ya, yb, yc, yd, wm, wb, wo, g, b)


def _route(scores, biased):
    s = [scores[e:e + 1, :] for e in range(N_EXPERTS)]
    b = [biased[e:e + 1, :] for e in range(N_EXPERTS)]
    gs = []
    for g in range(N_GROUPS):
        m = b[EXPERTS_PER_GROUP * g:EXPERTS_PER_GROUP * (g + 1)]
        best = None
        for i in range(EXPERTS_PER_GROUP):
            for j in range(i + 1, EXPERTS_PER_GROUP):
                ps = m[i] + m[j]
                best = ps if best is None else jnp.maximum(best, ps)
        gs.append(best)
    gidx = jnp.zeros_like(gs[0], dtype=jnp.int32)
    gval = gs[0]
    for g in range(1, N_GROUPS):
        take = gs[g] > gval
        gidx = jnp.where(take, g, gidx)
        gval = jnp.where(take, gs[g], gval)
    w = []
    for e in range(N_EXPERTS):
        g = e // EXPERTS_PER_GROUP
        beaten = jnp.zeros_like(gidx)
        for j in range(EXPERTS_PER_GROUP * g, EXPERTS_PER_GROUP * (g + 1)):
            if j == e:
                continue
            wins = (b[j] >= b[e]) if j < e else (b[j] > b[e])
            beaten = beaten + wins.astype(jnp.int32)
        sel = (gidx == g) & (beaten < 2)
        w.append(jnp.where(sel, s[e], 0.0))
    tot = w[0]
    for e in range(1, N_EXPERTS):
        tot = tot + w[e]
    return [w[e] / tot for e in range(N_EXPERTS)], gidx


def _moe_kernel(x_ref, xb_ref, wr_ref, rb_ref, su_ref, wgu_ref, wd_ref, g_ref, b_ref,
                o_ref, ob_ref, acc_ref, tr_ref, slotr_ref, slotc_ref, chl_ref, cnt_ref):
    g = pl.program_id(1)
    tm = x_ref.shape[0]
    cc = MOE_CHUNK

    @pl.when(g == 0)
    def _():
        logits = lax.dot_general(wr_ref[...], x_ref[...], (((1,), (1,)), ((), ())),
                                 precision=lax.Precision.HIGHEST, preferred_element_type=F32)
        scores = jax.nn.sigmoid(logits)
        rows, gidx = _route(scores, scores + rb_ref[...])
        member = [(gidx == k).astype(F32) for k in range(N_GROUPS)]
        mm = jnp.concatenate(member + [jnp.zeros((SUBLANES - N_GROUPS, tm), F32)], axis=0)
        rank = jnp.dot(mm.astype(BF16), su_ref[...], preferred_element_type=F32)
        slot = jnp.where(mm > 0.0, rank, -1.0)
        slotr_ref[...] = slot
        for k in range(N_GROUPS):
            cnt_ref[k] = jnp.sum(member[k]).astype(jnp.int32)
        tr_ref[...] = jnp.zeros_like(tr_ref)
        for k in range(N_EXPERTS):
            tr_ref[k:k + 1, :] = rows[k]
        tr_ref[N_EXPERTS:N_EXPERTS + SUBLANES, :] = slot
        tc = tr_ref[...].T
        hi = tc.astype(BF16)
        chl_ref[:, :LANES] = hi
        chl_ref[:, LANES:] = (tc - hi.astype(F32)).astype(BF16)
        for k in range(N_GROUPS):
            slotc_ref[k] = jnp.broadcast_to(tc[:, N_EXPERTS + k:N_EXPERTS + k + 1], (tm, LANES))
        acc_ref[...] = jnp.zeros_like(acc_ref)

    slot_row = slotr_ref[pl.ds(g, 1), :]

    def expert_pass(base, rows):
        slot_col = jnp.concatenate([slotc_ref[g]] * (rows // LANES), axis=1)
        lane = lax.broadcasted_iota(jnp.int32, (rows, LANES), 1)
        row_i = lax.broadcasted_iota(jnp.int32, (rows, tm), 0).astype(F32)
        col_i = lax.broadcasted_iota(jnp.int32, (tm, rows), 1).astype(F32)
        basef = base.astype(F32)
        gather = (slot_row == row_i + basef).astype(BF16)
        xg = jnp.dot(gather, xb_ref[...], preferred_element_type=F32).astype(BF16)
        cw2 = jnp.dot(gather, chl_ref[...], preferred_element_type=F32)
        cw = cw2[:, :LANES] + cw2[:, LANES:]
        yg = None
        for k in range(EXPERTS_PER_GROUP):
            hgu = jnp.dot(xg, wgu_ref[0, k], preferred_element_type=F32)
            h = _silu(hgu[:, :D_EXPERT]) * hgu[:, D_EXPERT:]
            ce = jnp.sum(jnp.where(lane == g * EXPERTS_PER_GROUP + k, cw, 0.0), axis=-1, keepdims=True)
            term = ce * _bdot(h, wd_ref[0, k])
            yg = term if yg is None else yg + term
        scatter = (slot_col == col_i + basef).astype(BF16)
        acc_ref[...] += jnp.dot(scatter, yg.astype(BF16), preferred_element_type=F32)

    half = cc // 2
    n = cnt_ref[g]
    rem = jnp.bitwise_and(n, cc - 1)
    small = jnp.logical_and(rem > 0, rem <= half).astype(jnp.int32)
    n_full = jnp.right_shift(n + cc - 1, MOE_CHUNK_LOG2) - small

    def full_pass(ci, carry):
        expert_pass(ci * cc, cc)
        return carry

    def half_pass(ci, carry):
        expert_pass(n_full * cc, half)
        return carry

    lax.fori_loop(0, n_full, full_pass, 0)
    lax.fori_loop(0, small, half_pass, 0)

    @pl.when(g == N_GROUPS - 1)
    def _():
        out = _layer_norm(DN_ALPHA * x_ref[...] + acc_ref[...], g_ref[...], b_ref[...])
        o_ref[...] = out
        ob_ref[...] = out.astype(BF16)


def _moe_call(x, xb, wr, rb, su, wgu, wd, g, b):
    t, d = x.shape
    tm = MOE_TILE
    epg = EXPERTS_PER_GROUP
    tok = lambda mode=None: pl.BlockSpec((tm, d), lambda i, k: (i, 0), pipeline_mode=mode)
    const = lambda shape, mode=None: pl.BlockSpec(shape, lambda i, k: (0,) * len(shape),
                                                  pipeline_mode=mode)
    return pl.pallas_call(
        _moe_kernel,
        grid=(t // tm, N_GROUPS),
        in_specs=[tok(pl.Buffered(1)), tok(pl.Buffered(1)), const(wr.shape), const(rb.shape),
                  const(su.shape, pl.Buffered(1)),
                  pl.BlockSpec((1, epg, d, 2 * D_EXPERT), lambda i, k: (k, 0, 0, 0)),
                  pl.BlockSpec((1, epg, D_EXPERT, d), lambda i, k: (k, 0, 0, 0)),
                  const(g.shape), const(b.shape)],
        out_specs=[tok(), tok()],
        out_shape=[jax.ShapeDtypeStruct((t, d), F32), jax.ShapeDtypeStruct((t, d), BF16)],
        scratch_shapes=[pltpu.VMEM((tm, d), F32),
                        pltpu.VMEM((LANES, tm), F32),
                        pltpu.VMEM((SUBLANES, tm), F32),
                        pltpu.VMEM((N_GROUPS, tm, LANES), F32),
                        pltpu.VMEM((tm, 2 * LANES), BF16),
                        pltpu.SMEM((N_GROUPS,), jnp.int32)],
        compiler_params=pltpu.CompilerParams(dimension_semantics=("arbitrary", "arbitrary"),
                                             vmem_limit_bytes=VMEM_LIMIT),
        name="moe",
    )(x, xb, wr, rb, su, wgu, wd, g, b)


def _block_diag(w):
    g, n, _ = w.shape
    eye = jnp.eye(g, dtype=w.dtype)
    return (eye[:, None, :, None] * w[:, :, None, :]).reshape(g * n, g * n)


def _per_head(w, off, n_heads, width, parts):
    cols = [w[:, off + p * n_heads * width: off + (p + 1) * n_heads * width]
            .reshape(w.shape[0], n_heads, width) for p in parts]
    return jnp.transpose(jnp.concatenate(cols, axis=2), (1, 0, 2))


def _row(v):
    return v.reshape(1, -1).astype(F32)


def kernel(x, w_in, conv_a_w, conv_a_b, rg_w_a, rg_b_a, rg_w_x, rg_b_x, rg_lambda, gdn_conv_w, gdn_a_log, gdn_dt_bias, gdn_norm_w, gla_w_gate_up, gla_b_gate, gla_norm_w, w_branch, w_out, ln1_g, ln1_b, w_router, router_bias, w_gate, w_up, w_down, ln2_g, ln2_b):
    bsz, s, d = x.shape
    t = bsz * s
    c = GDN_CHUNK
    ii = jnp.arange(SB_BLOCK)
    u2 = jnp.concatenate([(ii[:, None] >= ii[None, :]).astype(BF16),
                          jnp.ones((SB_BLOCK, SB_BLOCK), BF16)], axis=1)
    u2 = jnp.concatenate([u2, u2], axis=0)
    jj = jnp.arange(c)
    tri = (jj[:, None] >= jj[None, :]).astype(BF16)
    wr = w_router.astype(F32).T
    rb = router_bias.astype(F32).reshape(N_EXPERTS, 1)
    kk = jnp.arange(MOE_TILE)
    su = (kk[:, None] < kk[None, :]).astype(BF16)

    xf = x.reshape(t, d)
    xb = xf.astype(BF16)
    for l in range(DEPTH):
        wl = w_in[l]
        wlb = wl.astype(BF16)
        xb3 = xb.reshape(bsz, s, d)

        wg = jnp.concatenate([_block_diag(rg_w_a[l]), _block_diag(rg_w_x[l])], axis=1).astype(BF16)
        bg = jnp.concatenate([rg_b_a[l], rg_b_x[l]]).reshape(1, -1).astype(F32)
        ya = _lru_call(xb3, wlb[:, _OFF_A:_OFF_A + LRU_WIDTH], conv_a_w[l].astype(F32),
                       _row(conv_a_b[l]), wg, bg, _row(rg_lambda[l]))

        w_gdn = jnp.concatenate([_per_head(wlb, _OFF_GDN_QKV, GDN_HEADS, GDN_DK, (0, 1, 2)),
                                 _per_head(wlb, _OFF_GDN_GATE, GDN_HEADS, GDN_DV, (0,))], axis=2)
        wbd = jnp.stack([wlb[:, _OFF_GDN_BETA:_OFF_GDN_BETA + GDN_HEADS],
                         wlb[:, _OFF_GDN_DECAY:_OFF_GDN_DECAY + GDN_HEADS]], axis=2)
        wbd = jnp.pad(jnp.transpose(wbd, (1, 0, 2)), ((0, 0), (0, 0), (0, LANES - 2)))
        cw_gdn = jnp.transpose(gdn_conv_w[l].astype(F32).reshape(CONV_WIDTH, 3, GDN_HEADS, GDN_DK),
                               (2, 0, 1, 3)).reshape(GDN_HEADS, CONV_WIDTH, 3 * GDN_DK)
        hp = jnp.zeros((GDN_HEADS, SUBLANES, GDN_DK), F32)
        hp = hp.at[:, 0, :].set(gdn_a_log[l].astype(F32)[:, None])
        hp = hp.at[:, 1, :].set(gdn_dt_bias[l].astype(F32)[:, None])
        yb = _gdn_call(xb3, w_gdn, wbd, cw_gdn, hp, _row(gdn_norm_w[l]), tri)

        yc = _sb_call(xb3, wlb[:, _OFF_SB:_OFF_SB + 3 * SB_HEADS * SB_DH], u2)

        q_off = _OFF_GLA_QKV
        w_gla = jnp.concatenate([
            _per_head(wlb, q_off, GLA_HEADS, GLA_DK, (0, 1)),
            _per_head(wlb, q_off + 2 * GLA_HEADS * GLA_DK, GLA_HEADS, GLA_DV, (0,)),
            _per_head(wlb, _OFF_GLA_GATE, GLA_HEADS, GLA_DV, (0,))], axis=2)
        wlr = jnp.pad(wlb[:, _OFF_GLA_LR:_OFF_GLA_LR + GLA_GATE_RANK],
                      ((0, 0), (0, LANES - GLA_GATE_RANK)))
        wgu = jnp.pad(jnp.transpose(gla_w_gate_up[l].reshape(GLA_GATE_RANK, GLA_HEADS, GLA_DK), (1, 0, 2)),
                      ((0, 0), (0, LANES - GLA_GATE_RANK), (0, 0))).astype(BF16)
        bgl = gla_b_gate[l].astype(F32).reshape(GLA_HEADS, 1, GLA_DK)
        yd = _gla_call(xb3, w_gla, wlr, wgu, bgl, _row(gla_norm_w[l]), tri)

        xf, xb = _merge_call(xf, xb, ya.reshape(t, -1), yb.reshape(t, -1), yc.reshape(t, -1),
                             yd.reshape(t, -1), wlb[:, _OFF_MERGE:], w_branch[l].astype(BF16),
                             w_out[l].astype(BF16), _row(ln1_g[l]), _row(ln1_b[l]))

        w_gu = jnp.concatenate([w_gate[l].astype(BF16), w_up[l].astype(BF16)], axis=2)
        w_gu = w_gu.reshape(N_GROUPS, EXPERTS_PER_GROUP, d, 2 * D_EXPERT)
        w_dn = w_down[l].astype(BF16).reshape(N_GROUPS, EXPERTS_PER_GROUP, D_EXPERT, d)
        xf, xb = _moe_call(xf, xb, wr, rb, su, w_gu, w_dn, _row(ln2_g[l]), _row(ln2_b[l]))
    return xf.reshape(bsz, s, d)
```

```python
import functools

import jax
import jax.numpy as jnp
from jax import lax
from jax.experimental import pallas as pl
from jax.experimental.pallas import tpu as pltpu

F32 = jnp.float32
BF16 = jnp.bfloat16

D_MODEL = 1024
DEPTH = 4
LRU_WIDTH = 512
LRU_BLOCKS = 8
LRU_C = 8.0
CONV_WIDTH = 4
GDN_HEADS = 4
GDN_DK = 128
GDN_DV = 128
GDN_CHUNK = 64
SB_HEADS = 8
SB_DH = 64
SB_BLOCK = 128
GLA_HEADS = 4
GLA_DK = 128
GLA_DV = 128
GLA_GATE_RANK = 16
GLA_TAU = 16.0
GLA_CHUNK = 64
N_BRANCHES = 4
BRANCH_WIDTH = 512
N_EXPERTS = 16
N_GROUPS = 4
EXPERTS_PER_GROUP = 4
D_EXPERT = 512
LN_EPS = 1e-5
NORM_EPS = 1e-6
DN_ALPHA = (2 * DEPTH) ** 0.25
LOG2E = 1.4426950408889634
SB_UNDERFLOW_LOG2 = 150.0
SB_BOUND_SLACK = 1.02

_OFF_A = 0
_OFF_GDN_QKV = _OFF_A + LRU_WIDTH
_OFF_GDN_BETA = _OFF_GDN_QKV + 3 * GDN_HEADS * GDN_DK
_OFF_GDN_DECAY = _OFF_GDN_BETA + GDN_HEADS
_OFF_GDN_GATE = _OFF_GDN_DECAY + GDN_HEADS
_OFF_SB = _OFF_GDN_GATE + GDN_HEADS * GDN_DV
_OFF_GLA_QKV = _OFF_SB + 3 * SB_HEADS * SB_DH
_OFF_GLA_LR = _OFF_GLA_QKV + GLA_HEADS * (2 * GLA_DK + GLA_DV)
_OFF_GLA_GATE = _OFF_GLA_LR + GLA_GATE_RANK
_OFF_MERGE = _OFF_GLA_GATE + GLA_HEADS * GLA_DV

LANES = 128
SUBLANES = 8
ROW_TILE = 256
LRU_SUBTILES = 2
MERGE_TILE = 512
CHUNK_UNROLL = 16
GDN_UNROLL = 16
MOE_TILE = 1024
MOE_CHUNK_LOG2 = 8
MOE_CHUNK = 1 << MOE_CHUNK_LOG2
VMEM_LIMIT = 56 * 1024 * 1024


def _bdot(a, b):
    return jnp.dot(a.astype(BF16), b.astype(BF16), preferred_element_type=F32)


def _bdot_nt(a, b):
    return lax.dot_general(a.astype(BF16), b.astype(BF16), (((1,), (1,)), ((), ())),
                           preferred_element_type=F32)


def _bdot_tn(a, b):
    return lax.dot_general(a.astype(BF16), b.astype(BF16), (((0,), (0,)), ((), ())),
                           preferred_element_type=F32)


def _split_dot_nt(a_bf16, b):
    hi = b.astype(BF16)
    lo = (b - hi.astype(F32)).astype(BF16)
    dn = (((1,), (1,)), ((), ()))
    return (lax.dot_general(a_bf16, hi, dn, preferred_element_type=F32)
            + lax.dot_general(a_bf16, lo, dn, preferred_element_type=F32))


def _silu(x):
    return x * jax.nn.sigmoid(x)


def _layer_norm(x, g, b):
    mu = jnp.mean(x, axis=-1, keepdims=True)
    xc = x - mu
    var = jnp.mean(xc * xc, axis=-1, keepdims=True)
    return xc * lax.rsqrt(var + LN_EPS) * g + b


def _shift_rows(u, tail, s, row8):
    ur = pltpu.roll(u, s, 0)
    top = jnp.where(row8 >= s, ur[:SUBLANES], pltpu.roll(tail, s, 0))
    return jnp.concatenate([top, ur[SUBLANES:]], axis=0)


def _causal_conv(u, tail, cw_ref, row8):
    y = u * cw_ref[CONV_WIDTH - 1:CONV_WIDTH, :]
    for s in range(1, CONV_WIDTH):
        k = CONV_WIDTH - 1 - s
        y = y + _shift_rows(u, tail, s, row8) * cw_ref[k:k + 1, :]
    return y


def _lru_kernel(xb_ref, w_ref, cw_ref, cb_ref, wg_ref, bg_ref, lam_ref, o_ref, tail_ref, h_ref):
    t = pl.program_id(1)
    tt = ROW_TILE
    nsub = xb_ref.shape[1] // tt
    subs = range(nsub)
    w = LRU_WIDTH

    @pl.when(t == 0)
    def _():
        tail_ref[...] = jnp.zeros_like(tail_ref)
        h_ref[...] = jnp.zeros_like(h_ref)

    u = [jnp.dot(xb_ref[0, i * tt:(i + 1) * tt, :], w_ref[...], preferred_element_type=F32)
         for i in subs]
    row = lax.broadcasted_iota(jnp.int32, (tt, w), 0)
    row8 = lax.broadcasted_iota(jnp.int32, (SUBLANES, w), 0)
    tails = [tail_ref[...]] + [u[i][tt - SUBLANES:, :] for i in subs[:-1]]
    xc = [_causal_conv(u[i], tails[i], cw_ref, row8) + cb_ref[...] for i in subs]
    tail_ref[...] = u[nsub - 1][tt - SUBLANES:, :]

    gates = [_bdot(xc[i], wg_ref[...]) + bg_ref[...] for i in subs]
    sp = jax.nn.softplus(-lam_ref[...])
    log_a = [(-LRU_C) * jax.nn.sigmoid(gates[i][:, :w]) * sp for i in subs]
    a = [jnp.exp(log_a[i]) for i in subs]
    mult = [jnp.sqrt(-jnp.tanh(log_a[i]) * (a[i] * a[i] + 1.0)) for i in subs]
    mult[0] = jnp.where(row + t == 0, 1.0, mult[0])
    b = [mult[i] * jax.nn.sigmoid(gates[i][:, w:]) * xc[i] for i in subs]

    d = 1
    while d < SUBLANES:
        keep = jnp.bitwise_and(row, SUBLANES - 1) >= d
        a_sh = [jnp.where(keep, pltpu.roll(a[i], d, 0), 1.0) for i in subs]
        b_sh = [jnp.where(keep, pltpu.roll(b[i], d, 0), 0.0) for i in subs]
        b = [a[i] * b_sh[i] + b[i] for i in subs]
        a = [a[i] * a_sh[i] for i in subs]
        d *= 2
    carry = h_ref[...]
    for i in subs:
        hs = []
        for gi in range(tt // SUBLANES):
            rows = slice(gi * SUBLANES, (gi + 1) * SUBLANES)
            hg = a[i][rows, :] * carry + b[i][rows, :]
            carry = jnp.broadcast_to(hg[SUBLANES - 1:SUBLANES, :], (SUBLANES, w))
            hs.append(hg)
        o_ref[0, i * tt:(i + 1) * tt, :] = jnp.concatenate(hs, axis=0).astype(o_ref.dtype)
    h_ref[...] = carry


def _lru_call(xb, w, cw, cb, wg, bg, lam):
    bsz, s, d = xb.shape
    tt = LRU_SUBTILES * ROW_TILE
    const = lambda shape: pl.BlockSpec(shape, lambda b, t: (0,) * len(shape))
    return pl.pallas_call(
        _lru_kernel,
        grid=(bsz, s // tt),
        in_specs=[pl.BlockSpec((1, tt, d), lambda b, t: (b, t, 0)),
                  const(w.shape), const(cw.shape), const(cb.shape), const(wg.shape),
                  const(bg.shape), const(lam.shape)],
        out_specs=pl.BlockSpec((1, tt, LRU_WIDTH), lambda b, t: (b, t, 0)),
        out_shape=jax.ShapeDtypeStruct((bsz, s, LRU_WIDTH), BF16),
        scratch_shapes=[pltpu.VMEM((SUBLANES, LRU_WIDTH), F32), pltpu.VMEM((SUBLANES, LRU_WIDTH), F32)],
        compiler_params=pltpu.CompilerParams(dimension_semantics=("arbitrary", "arbitrary"),
                                             vmem_limit_bytes=VMEM_LIMIT),
        name="mixer_lru",
    )(xb, w, cw, cb, wg, bg, lam)


def _sb_kernel(xb_ref, w_ref, u2_ref, hsel_ref, o_ref, q_s, k_s, v_s, acc_s, run_s):
    s = xb_ref.shape[1]
    blk = SB_BLOCK
    dh = SB_DH
    nh = SB_HEADS
    hd = nh * dh

    def colmax(x):
        m = x[:SUBLANES]
        for gi in range(1, ROW_TILE // SUBLANES):
            m = jnp.maximum(m, x[gi * SUBLANES:(gi + 1) * SUBLANES])
        return m

    def proj(t, carry):
        cq, ck = carry
        r0 = pl.multiple_of(t * ROW_TILE, ROW_TILE)
        qkv = jnp.dot(xb_ref[0, pl.ds(r0, ROW_TILE), :], w_ref[...], preferred_element_type=F32)
        qb = (qkv[:, :hd] * (dh ** -0.5 * LOG2E)).astype(BF16)
        kb = qkv[:, hd:2 * hd].astype(BF16)
        for h in range(nh):
            q_s[h, pl.ds(r0, ROW_TILE), :] = qb[:, h * dh:(h + 1) * dh]
            k_s[h, pl.ds(r0, ROW_TILE), :] = kb[:, h * dh:(h + 1) * dh]
            v_s[h, pl.ds(r0, ROW_TILE), :] = qkv[:, 2 * hd + h * dh:2 * hd + (h + 1) * dh].astype(BF16)
        qf = qb.astype(F32)
        kf = kb.astype(F32)
        nq = _bdot(qf * qf, hsel_ref[...])
        nk = _bdot(kf * kf, hsel_ref[...])
        return jnp.maximum(cq, colmax(nq)), jnp.maximum(ck, colmax(nk))

    zeros8 = jnp.zeros((SUBLANES, LANES), F32)
    cq, ck = lax.fori_loop(0, s // ROW_TILE, proj, (zeros8, zeros8))

    zb2 = jnp.max(cq, axis=0, keepdims=True) * jnp.max(ck, axis=0, keepdims=True)
    stop_below = -(SB_UNDERFLOW_LOG2 + SB_BOUND_SLACK * jnp.max(jnp.sqrt(zb2)))

    row = lax.broadcasted_iota(jnp.int32, (blk, blk), 0)
    col = lax.broadcasted_iota(jnp.int32, (blk, blk), 1)
    u2 = u2_ref[...]

    def max_run():
        m = run_s[0]
        for h in range(1, nh):
            m = jnp.maximum(m, run_s[h])
        return jnp.max(m)

    def qloop(qi, c):
        q0 = pl.multiple_of(qi * blk, blk)
        acc_s[...] = jnp.zeros_like(acc_s)
        run_s[...] = jnp.zeros_like(run_s)

        def blocks(k0s, diagonal):
            ch = [(h, j) for j in range(len(k0s)) for h in range(nh)]
            nt = (((1,), (1,)), ((), ()))
            z = {c_: lax.dot_general(q_s[c_[0], pl.ds(q0, blk), :], k_s[c_[0], pl.ds(k0s[c_[1]], blk), :],
                                     nt, preferred_element_type=F32) for c_ in ch}
            nz = {c_: -z[c_] for c_ in ch}
            lk = {c_: jnp.minimum(nz[c_], 0.0) - jnp.log2(1.0 + jnp.exp2(jnp.minimum(z[c_], nz[c_])))
                  for c_ in ch}
            lk = {c_: jnp.where(col < row, lk[c_], 0.0) if diagonal[c_[1]] else lk[c_] for c_ in ch}
            hi = {c_: lk[c_].astype(BF16) for c_ in ch}
            lo = {c_: (lk[c_] - hi[c_].astype(F32)).astype(BF16) for c_ in ch}
            cs2 = {c_: jnp.dot(jnp.concatenate([hi[c_], lo[c_]], axis=1), u2, preferred_element_type=F32)
                   for c_ in ch}
            run = {}
            for h in range(nh):
                r = run_s[h]
                for j in range(len(k0s)):
                    run[(h, j)] = r
                    r = r + cs2[(h, j)][:, blk:]
                run_s[h] = r
            p = {c_: jnp.exp2(z[c_] + cs2[c_][:, :blk] + run[c_]) for c_ in ch}
            p = {c_: jnp.where(col < row, p[c_], 0.0) if diagonal[c_[1]] else p[c_] for c_ in ch}
            pv = {c_: jnp.dot(p[c_].astype(BF16), v_s[c_[0], pl.ds(k0s[c_[1]], blk), :],
                              preferred_element_type=F32) for c_ in ch}
            for h in range(nh):
                tot = pv[(h, 0)]
                for j in range(1, len(k0s)):
                    tot = tot + pv[(h, j)]
                acc_s[h] += tot

        blocks([q0], (True,))
        npairs = jnp.right_shift(qi, 1)

        def more(c2):
            return jnp.logical_and(c2[0] < npairs, c2[1] > stop_below)

        def pair(c2):
            i = c2[0]
            ka = pl.multiple_of((qi - 1 - 2 * i) * blk, blk)
            kb = pl.multiple_of((qi - 2 - 2 * i) * blk, blk)
            blocks([ka, kb], (False, False))
            return i + 1, max_run()

        _, top = lax.while_loop(more, pair, (jnp.int32(0), max_run()))

        def last(i, c2):
            blocks([0], (False,))
            return c2

        live = jnp.logical_and(jnp.bitwise_and(qi, 1) == 1, top > stop_below)
        lax.fori_loop(0, live.astype(jnp.int32), last, 0)
        for h in range(nh):
            o_ref[0, pl.ds(q0, blk), h * dh:(h + 1) * dh] = acc_s[h].astype(o_ref.dtype)
        return c

    lax.fori_loop(0, s // blk, qloop, 0)


def _sb_call(xb, w, u2):
    bsz, s, d = xb.shape
    hsel = (jnp.arange(SB_HEADS * SB_DH)[:, None] // SB_DH == jnp.arange(LANES)[None, :]).astype(BF16)
    return pl.pallas_call(
        _sb_kernel,
        grid=(bsz,),
        in_specs=[pl.BlockSpec((1, s, d), lambda b: (b, 0, 0)),
                  pl.BlockSpec(w.shape, lambda b: (0, 0)),
                  pl.BlockSpec(u2.shape, lambda b: (0, 0)),
                  pl.BlockSpec(hsel.shape, lambda b: (0, 0))],
        out_specs=pl.BlockSpec((1, s, SB_HEADS * SB_DH), lambda b: (b, 0, 0)),
        out_shape=jax.ShapeDtypeStruct((bsz, s, SB_HEADS * SB_DH), BF16),
        scratch_shapes=[pltpu.VMEM((SB_HEADS, s, SB_DH), BF16)] * 3
        + [pltpu.VMEM((SB_HEADS, SB_BLOCK, SB_DH), F32), pltpu.VMEM((SB_HEADS, SB_BLOCK, SB_BLOCK), F32)],
        compiler_params=pltpu.CompilerParams(dimension_semantics=("arbitrary",),
                                             vmem_limit_bytes=VMEM_LIMIT),
        name="mixer_sb",
    )(xb, w, u2, hsel)


def _gla_kernel(xb_ref, w_ref, wlr_ref, wgu_ref, bg_ref, nw_ref, tri_ref, o_ref,
                q_s, k_s, v_s, g_s, gate_s, oi_s, qd_s, m_s, eb_s):
    s = xb_ref.shape[1]
    c = GLA_CHUNK
    dk = GLA_DK
    nck = s // c

    def proj(t, carry):
        us = range(2)
        r0 = [pl.multiple_of((2 * t + i) * ROW_TILE, ROW_TILE) for i in us]
        x = [xb_ref[0, pl.ds(r0[i], ROW_TILE), :] for i in us]
        lr = [jnp.dot(x[i], wlr_ref[...], preferred_element_type=F32) for i in us]
        p = [jnp.dot(x[i], w_ref[0], preferred_element_type=F32) for i in us]
        gpre = [_bdot(lr[i], wgu_ref[0]) + bg_ref[0] for i in us]
        for i in us:
            rows = pl.ds(r0[i], ROW_TILE)
            g_s[rows, :] = jax.nn.log_sigmoid(gpre[i]) * (1.0 / GLA_TAU)
            q_s[rows, :] = p[i][:, :dk] * dk ** -0.5
            k_s[rows, :] = p[i][:, dk:2 * dk]
            v_s[rows, :] = p[i][:, 2 * dk:3 * dk]
            gate_s[rows, :] = p[i][:, 3 * dk:]
        return carry

    lax.fori_loop(0, s // (2 * ROW_TILE), proj, 0)

    row = lax.broadcasted_iota(jnp.int32, (c, c), 0)
    col = lax.broadcasted_iota(jnp.int32, (c, c), 1)
    tri = tri_ref[...]

    def intra(grp, carry):
        u = range(CHUNK_UNROLL)
        ci = [grp * CHUNK_UNROLL + j for j in u]
        r0 = [pl.multiple_of(ci[j] * c, c) for j in u]
        q = [q_s[pl.ds(r0[j], c), :] for j in u]
        k = [k_s[pl.ds(r0[j], c), :] for j in u]
        v = [v_s[pl.ds(r0[j], c), :] for j in u]
        bc = [_split_dot_nt_left(tri, g_s[pl.ds(r0[j], c), :]) for j in u]
        blast = [bc[j][c - 1:c, :] for j in u]
        qd = [q[j] * jnp.exp(bc[j]) for j in u]
        ki = [k[j] * jnp.exp(-bc[j]) for j in u]
        kd = [k[j] * jnp.exp(blast[j] - bc[j]) for j in u]
        attn = [jnp.where(col <= row, _bdot_nt(qd[j], ki[j]), 0.0) for j in u]
        oi = [_bdot(attn[j], v[j]) for j in u]
        m = [_bdot_tn(v[j], kd[j]) for j in u]
        for j in u:
            oi_s[pl.ds(r0[j], c), :] = oi[j]
            qd_s[pl.ds(r0[j], c), :] = qd[j].astype(BF16)
            m_s[ci[j]] = m[j]
            eb_s[pl.ds(ci[j], 1), :] = jnp.exp(blast[j])
        return carry

    lax.fori_loop(0, nck // CHUNK_UNROLL, intra, 0)

    def inter(grp, st):
        for j in range(CHUNK_UNROLL):
            ci = grp * CHUNK_UNROLL + j
            r0 = pl.multiple_of(ci * c, c)
            o = oi_s[pl.ds(r0, c), :] + lax.dot_general(
                qd_s[pl.ds(r0, c), :], st.astype(BF16), (((1,), (1,)), ((), ())),
                preferred_element_type=F32)
            st = st * eb_s[pl.ds(ci, 1), :] + m_s[ci]
            ms = jnp.mean(o * o, axis=-1, keepdims=True)
            y = o * lax.rsqrt(ms + NORM_EPS) * nw_ref[...] * _silu(gate_s[pl.ds(r0, c), :])
            o_ref[0, pl.ds(r0, c), :] = y.astype(o_ref.dtype)
        return st

    lax.fori_loop(0, nck // CHUNK_UNROLL, inter, jnp.zeros((GLA_DV, dk), F32))


def _split_dot_nt_left(tri_bf16, x):
    hi = x.astype(BF16)
    lo = (x - hi.astype(F32)).astype(BF16)
    return (jnp.dot(tri_bf16, hi, preferred_element_type=F32)
            + jnp.dot(tri_bf16, lo, preferred_element_type=F32))


def _gla_call(xb, w, wlr, wgu, bg, nw, tri):
    bsz, s, d = xb.shape
    return pl.pallas_call(
        _gla_kernel,
        grid=(bsz, GLA_HEADS),
        in_specs=[pl.BlockSpec((1, s, d), lambda b, h: (b, 0, 0)),
                  pl.BlockSpec((1, d, 4 * GLA_DK), lambda b, h: (h, 0, 0)),
                  pl.BlockSpec(wlr.shape, lambda b, h: (0, 0)),
                  pl.BlockSpec((1, LANES, GLA_DK), lambda b, h: (h, 0, 0)),
                  pl.BlockSpec((1, 1, GLA_DK), lambda b, h: (h, 0, 0)),
                  pl.BlockSpec(nw.shape, lambda b, h: (0, 0)),
                  pl.BlockSpec(tri.shape, lambda b, h: (0, 0))],
        out_specs=pl.BlockSpec((1, s, GLA_DV), lambda b, h: (b, 0, h)),
        out_shape=jax.ShapeDtypeStruct((bsz, s, GLA_HEADS * GLA_DV), BF16),
        scratch_shapes=[pltpu.VMEM((s, GLA_DK), F32)] * 6
        + [pltpu.VMEM((s, GLA_DK), BF16), pltpu.VMEM((s // GLA_CHUNK, GLA_DV, GLA_DK), F32),
           pltpu.VMEM((s // GLA_CHUNK, GLA_DK), F32)],
        compiler_params=pltpu.CompilerParams(dimension_semantics=("arbitrary", "arbitrary"),
                                             vmem_limit_bytes=VMEM_LIMIT),
        name="mixer_gla",
    )(xb, w, wlr, wgu, bg, nw, tri)


def _gdn_kernel(xb_ref, w_ref, wbd_ref, cw_ref, hp_ref, nw_ref, tri_ref, o_ref,
                q_s, k_s, v_s, beta_s, g_s, gate_s, u_s, wq_s, kd_s, attn_s, egl_s, st_s):
    s = xb_ref.shape[1]
    c = GDN_CHUNK
    dk = GDN_DK
    dv = GDN_DV
    nck = s // c
    row8 = lax.broadcasted_iota(jnp.int32, (SUBLANES, 3 * dk), 0)
    row = lax.broadcasted_iota(jnp.int32, (c, c), 0)
    col = lax.broadcasted_iota(jnp.int32, (c, c), 1)
    eye = (row == col).astype(F32)
    tri = tri_ref[...]
    avg = jnp.full((c, dk), 1.0 / dk, BF16)

    def head(h, carry):
        def proj(t, tail):
            us = range(2)
            r0 = [pl.multiple_of((2 * t + i) * ROW_TILE, ROW_TILE) for i in us]
            x = [xb_ref[0, pl.ds(r0[i], ROW_TILE), :] for i in us]
            p = [jnp.dot(x[i], w_ref[h], preferred_element_type=F32) for i in us]
            bd = [jnp.dot(x[i], wbd_ref[h], preferred_element_type=F32) for i in us]
            u = [p[i][:, :3 * dk] for i in us]
            tails = [tail, u[0][ROW_TILE - SUBLANES:, :]]
            qkv = [_silu(_causal_conv(u[i], tails[i], cw_ref.at[h], row8)) for i in us]
            qq = [qkv[i][:, :dk] for i in us]
            kk = [qkv[i][:, dk:2 * dk] for i in us]
            qn = [lax.rsqrt(jnp.sum(qq[i] * qq[i], axis=-1, keepdims=True) + NORM_EPS) * dk ** -0.5 for i in us]
            kn = [lax.rsqrt(jnp.sum(kk[i] * kk[i], axis=-1, keepdims=True) + NORM_EPS) for i in us]
            for i in us:
                rows = pl.ds(r0[i], ROW_TILE)
                q_s[rows, :] = qq[i] * qn[i]
                k_s[rows, :] = kk[i] * kn[i]
                v_s[rows, :] = qkv[i][:, 2 * dk:]
                gate_s[h, rows, :] = p[i][:, 3 * dk:]
                beta = jax.nn.sigmoid(bd[i][:, 0:1])
                g = -jnp.exp(hp_ref[h, 0:1, :]) * jax.nn.softplus(bd[i][:, 1:2] + hp_ref[h, 1:2, :])
                beta_s[rows, :] = jnp.broadcast_to(beta, (ROW_TILE, dk))
                g_s[rows, :] = g
            return u[1][ROW_TILE - SUBLANES:, :]

        lax.fori_loop(0, s // (2 * ROW_TILE), proj, jnp.zeros((SUBLANES, 3 * dk), F32))

        def intra(grp, c1):
            js = range(GDN_UNROLL)
            ci = [grp * GDN_UNROLL + j for j in js]
            r0 = [pl.multiple_of(ci[j] * c, c) for j in js]
            q = [q_s[pl.ds(r0[j], c), :] for j in js]
            k = [k_s[pl.ds(r0[j], c), :] for j in js]
            v = [v_s[pl.ds(r0[j], c), :] for j in js]
            beta = [beta_s[pl.ds(r0[j], c), :] for j in js]
            gc = [_split_dot_nt_left(tri, g_s[pl.ds(r0[j], c), :]) for j in js]
            gc_row = [_split_dot_nt(avg, gc[j]) for j in js]
            dec = [jnp.exp(jnp.minimum(gc[j][:, :c] - gc_row[j], 0.0)) for j in js]
            kb = [k[j] * beta[j] for j in js]
            qk = [_bdot_nt(jnp.concatenate([kb[j], q[j]], axis=0), k[j]) for j in js]
            a = [jnp.where(col < row, qk[j][:c] * dec[j], 0.0) for j in js]
            attn = [jnp.where(col <= row, qk[j][c:] * dec[j], 0.0) for j in js]
            tinv = [eye - a[j] for j in js]
            ak = a
            n = 1
            while n < c // 2:
                ak = [_bdot(ak[j], ak[j]) for j in js]
                tinv = [tinv[j] + _bdot(tinv[j], ak[j]) for j in js]
                n *= 2
            egc = [jnp.exp(gc[j]) for j in js]
            uw = [_bdot(tinv[j], jnp.concatenate([v[j] * beta[j], kb[j] * egc[j]], axis=1))
                  for j in js]
            for j in js:
                gl = gc[j][c - 1:c, :]
                u_s[h, pl.ds(r0[j], c), :] = uw[j][:, :dv]
                wq_s[h, ci[j]] = jnp.concatenate([uw[j][:, dv:], q[j] * egc[j]], axis=0).astype(BF16)
                kd_s[h, pl.ds(r0[j], c), :] = (k[j] * jnp.exp(gl - gc[j])).astype(BF16)
                attn_s[h, pl.ds(r0[j], c), :] = attn[j].astype(BF16)
                egl_s[h, pl.ds(ci[j], 1), :] = jnp.exp(gl)
            return c1

        lax.fori_loop(0, nck // GDN_UNROLL, intra, 0)
        return carry

    lax.fori_loop(0, GDN_HEADS, head, 0)

    st_s[...] = jnp.zeros_like(st_s)

    def inter(ci, c2):
        r0 = pl.multiple_of(ci * c, c)
        hs = range(GDN_HEADS)
        tn = (((0,), (0,)), ((), ()))
        st = [st_s[h] for h in hs]
        ws_qs = [jnp.dot(wq_s[h, ci], st[h].astype(BF16), preferred_element_type=F32) for h in hs]
        v_new = [(u_s[h, pl.ds(r0, c), :] - ws_qs[h][:c]).astype(BF16) for h in hs]
        o = [ws_qs[h][c:] + jnp.dot(attn_s[h, pl.ds(r0, c), :], v_new[h], preferred_element_type=F32)
             for h in hs]
        kv = [lax.dot_general(kd_s[h, pl.ds(r0, c), :], v_new[h], tn, preferred_element_type=F32)
              for h in hs]
        for h in hs:
            st_s[h] = st[h] * egl_s[h, pl.ds(ci, 1), :] + kv[h]
            ms = jnp.mean(o[h] * o[h], axis=-1, keepdims=True)
            y = o[h] * lax.rsqrt(ms + NORM_EPS) * nw_ref[...] * _silu(gate_s[h, pl.ds(r0, c), :])
            o_ref[0, pl.ds(r0, c), h * dv:(h + 1) * dv] = y.astype(o_ref.dtype)
        return c2

    lax.fori_loop(0, nck, inter, 0)


def _gdn_call(xb, w, wbd, cw, hp, nw, tri):
    bsz, s, d = xb.shape
    nh = GDN_HEADS
    nck = s // GDN_CHUNK
    const = lambda a: pl.BlockSpec(a.shape, lambda b: (0,) * a.ndim)
    return pl.pallas_call(
        _gdn_kernel,
        grid=(bsz,),
        in_specs=[pl.BlockSpec((1, s, d), lambda b: (b, 0, 0)),
                  const(w), const(wbd), const(cw), const(hp), const(nw), const(tri)],
        out_specs=pl.BlockSpec((1, s, nh * GDN_DV), lambda b: (b, 0, 0)),
        out_shape=jax.ShapeDtypeStruct((bsz, s, nh * GDN_DV), BF16),
        scratch_shapes=[pltpu.VMEM((s, GDN_DK), F32)] * 5
        + [pltpu.VMEM((nh, s, GDN_DV), F32),
           pltpu.VMEM((nh, s, GDN_DV), F32),
           pltpu.VMEM((nh, nck, 2 * GDN_CHUNK, GDN_DK), BF16),
           pltpu.VMEM((nh, s, GDN_DK), BF16),
           pltpu.VMEM((nh, s, GDN_CHUNK), BF16),
           pltpu.VMEM((nh, nck, GDN_DK), F32),
           pltpu.VMEM((nh, GDN_DK, GDN_DV), F32)],
        compiler_params=pltpu.CompilerParams(dimension_semantics=("arbitrary",),
                                             vmem_limit_bytes=VMEM_LIMIT),
        name="mixer_gdn",
    )(xb, w, wbd, cw, hp, nw, tri)


def _merge_kernel(x_ref, xb_ref, ya_ref, yb_ref, yc_ref, yd_ref, wm_ref, wb_ref, wo_ref,
                  g_ref, b_ref, o_ref, ob_ref):
    xb = xb_ref[...]
    merged = None
    for n, y_ref in enumerate((ya_ref, yb_ref, yc_ref, yd_ref)):
        gate = jax.nn.sigmoid(jnp.dot(xb, wm_ref[:, n * D_MODEL:(n + 1) * D_MODEL],
                                      preferred_element_type=F32))
        term = gate * jnp.dot(y_ref[...], wb_ref[n], preferred_element_type=F32)
        merged = term if merged is None else merged + term
    h = _bdot(merged, wo_ref[...])
    y = _layer_norm(DN_ALPHA * x_ref[...] + h, g_ref[...], b_ref[...])
    o_ref[...] = y
    ob_ref[...] = y.astype(BF16)


def _merge_call(x, xb, ya, yb, yc, yd, wm, wb, wo, g, b):
    t, d = x.shape
    tm = MERGE_TILE
    tok = lambda width: pl.BlockSpec((tm, width), lambda i: (i, 0))
    const = lambda shape: pl.BlockSpec(shape, lambda i: (0,) * len(shape),
                                       pipeline_mode=pl.Buffered(1))
    return pl.pallas_call(
        _merge_kernel,
        grid=(t // tm,),
        in_specs=[tok(d), tok(d), tok(BRANCH_WIDTH), tok(BRANCH_WIDTH), tok(BRANCH_WIDTH),
                  tok(BRANCH_WIDTH), const(wm.shape), const(wb.shape), const(wo.shape),
                  const(g.shape), const(b.shape)],
        out_specs=[tok(d), tok(d)],
        out_shape=[jax.ShapeDtypeStruct((t, d), F32), jax.ShapeDtypeStruct((t, d), BF16)],
        compiler_params=pltpu.CompilerParams(dimension_semantics=("arbitrary",),
                                             vmem_limit_bytes=VMEM_LIMIT),
        name="merge",
    )(x, xb, ya, yb, yc, yd, wm, wb, wo, g, b)


def _route(scores, biased):
    s = [scores[e:e + 1, :] for e in range(N_EXPERTS)]
    b = [biased[e:e + 1, :] for e in range(N_EXPERTS)]
    gs = []
    for g in range(N_GROUPS):
        m = b[EXPERTS_PER_GROUP * g:EXPERTS_PER_GROUP * (g + 1)]
        best = None
        for i in range(EXPERTS_PER_GROUP):
            for j in range(i + 1, EXPERTS_PER_GROUP):
                ps = m[i] + m[j]
                best = ps if best is None else jnp.maximum(best, ps)
        gs.append(best)
    gidx = jnp.zeros_like(gs[0], dtype=jnp.int32)
    gval = gs[0]
    for g in range(1, N_GROUPS):
        take = gs[g] > gval
        gidx = jnp.where(take, g, gidx)
        gval = jnp.where(take, gs[g], gval)
    w = []
    for e in range(N_EXPERTS):
        g = e // EXPERTS_PER_GROUP
        beaten = jnp.zeros_like(gidx)
        for j in range(EXPERTS_PER_GROUP * g, EXPERTS_PER_GROUP * (g + 1)):
            if j == e:
                continue
            wins = (b[j] >= b[e]) if j < e else (b[j] > b[e])
            beaten = beaten + wins.astype(jnp.int32)
        sel = (gidx == g) & (beaten < 2)
        w.append(jnp.where(sel, s[e], 0.0))
    tot = w[0]
    for e in range(1, N_EXPERTS):
        tot = tot + w[e]
    return [w[e] / tot for e in range(N_EXPERTS)], gidx


def _moe_kernel(x_ref, xb_ref, wr_ref, rb_ref, su_ref, wgu_ref, wd_ref, g_ref, b_ref,
                o_ref, ob_ref, acc_ref, tr_ref, slotr_ref, slotc_ref, chl_ref, cnt_ref):
    g = pl.program_id(1)
    tm = x_ref.shape[0]
    cc = MOE_CHUNK

    @pl.when(g == 0)
    def _():
        logits = lax.dot_general(wr_ref[...], x_ref[...], (((1,), (1,)), ((), ())),
                                 precision=lax.Precision.HIGHEST, preferred_element_type=F32)
        scores = jax.nn.sigmoid(logits)
        rows, gidx = _route(scores, scores + rb_ref[...])
        member = [(gidx == k).astype(F32) for k in range(N_GROUPS)]
        mm = jnp.concatenate(member + [jnp.zeros((SUBLANES - N_GROUPS, tm), F32)], axis=0)
        rank = jnp.dot(mm.astype(BF16), su_ref[...], preferred_element_type=F32)
        slot = jnp.where(mm > 0.0, rank, -1.0)
        slotr_ref[...] = slot
        for k in range(N_GROUPS):
            cnt_ref[k] = jnp.sum(member[k]).astype(jnp.int32)
        tr_ref[...] = jnp.zeros_like(tr_ref)
        for k in range(N_EXPERTS):
            tr_ref[k:k + 1, :] = rows[k]
        tr_ref[N_EXPERTS:N_EXPERTS + SUBLANES, :] = slot
        tc = tr_ref[...].T
        hi = tc.astype(BF16)
        chl_ref[:, :LANES] = hi
        chl_ref[:, LANES:] = (tc - hi.astype(F32)).astype(BF16)
        for k in range(N_GROUPS):
            slotc_ref[k] = jnp.broadcast_to(tc[:, N_EXPERTS + k:N_EXPERTS + k + 1], (tm, LANES))
        acc_ref[...] = jnp.zeros_like(acc_ref)

    slot_row = slotr_ref[pl.ds(g, 1), :]

    def expert_pass(base, rows):
        slot_col = jnp.concatenate([slotc_ref[g]] * (rows // LANES), axis=1)
        lane = lax.broadcasted_iota(jnp.int32, (rows, LANES), 1)
        row_i = lax.broadcasted_iota(jnp.int32, (rows, tm), 0).astype(F32)
        col_i = lax.broadcasted_iota(jnp.int32, (tm, rows), 1).astype(F32)
        basef = base.astype(F32)
        gather = (slot_row == row_i + basef).astype(BF16)
        xg = jnp.dot(gather, xb_ref[...], preferred_element_type=F32).astype(BF16)
        cw2 = jnp.dot(gather, chl_ref[...], preferred_element_type=F32)
        cw = cw2[:, :LANES] + cw2[:, LANES:]
        yg = None
        for k in range(EXPERTS_PER_GROUP):
            hgu = jnp.dot(xg, wgu_ref[0, k], preferred_element_type=F32)
            h = _silu(hgu[:, :D_EXPERT]) * hgu[:, D_EXPERT:]
            ce = jnp.sum(jnp.where(lane == g * EXPERTS_PER_GROUP + k, cw, 0.0), axis=-1, keepdims=True)
            term = ce * _bdot(h, wd_ref[0, k])
            yg = term if yg is None else yg + term
        scatter = (slot_col == col_i + basef).astype(BF16)
        acc_ref[...] += jnp.dot(scatter, yg.astype(BF16), preferred_element_type=F32)

    half = cc // 2
    n = cnt_ref[g]
    rem = jnp.bitwise_and(n, cc - 1)
    small = jnp.logical_and(rem > 0, rem <= half).astype(jnp.int32)
    n_full = jnp.right_shift(n + cc - 1, MOE_CHUNK_LOG2) - small

    def full_pass(ci, carry):
        expert_pass(ci * cc, cc)
        return carry

    def half_pass(ci, carry):
        expert_pass(n_full * cc, half)
        return carry

    lax.fori_loop(0, n_full, full_pass, 0)
    lax.fori_loop(0, small, half_pass, 0)

    @pl.when(g == N_GROUPS - 1)
    def _():
        out = _layer_norm(DN_ALPHA * x_ref[...] + acc_ref[...], g_ref[...], b_ref[...])
        o_ref[...] = out
        ob_ref[...] = out.astype(BF16)


def _moe_call(x, xb, wr, rb, su, wgu, wd, g, b):
    t, d = x.shape
    tm = MOE_TILE
    epg = EXPERTS_PER_GROUP
    tok = lambda mode=None: pl.BlockSpec((tm, d), lambda i, k: (i, 0), pipeline_mode=mode)
    const = lambda shape, mode=None: pl.BlockSpec(shape, lambda i, k: (0,) * len(shape),
                                                  pipeline_mode=mode)
    return pl.pallas_call(
        _moe_kernel,
        grid=(t // tm, N_GROUPS),
        in_specs=[tok(pl.Buffered(1)), tok(pl.Buffered(1)), const(wr.shape), const(rb.shape),
                  const(su.shape, pl.Buffered(1)),
                  pl.BlockSpec((1, epg, d, 2 * D_EXPERT), lambda i, k: (k, 0, 0, 0)),
                  pl.BlockSpec((1, epg, D_EXPERT, d), lambda i, k: (k, 0, 0, 0)),
                  const(g.shape), const(b.shape)],
        out_specs=[tok(), tok()],
        out_shape=[jax.ShapeDtypeStruct((t, d), F32), jax.ShapeDtypeStruct((t, d), BF16)],
        scratch_shapes=[pltpu.VMEM((tm, d), F32),
                        pltpu.VMEM((LANES, tm), F32),
                        pltpu.VMEM((SUBLANES, tm), F32),
                        pltpu.VMEM((N_GROUPS, tm, LANES), F32),
                        pltpu.VMEM((tm, 2 * LANES), BF16),
                        pltpu.SMEM((N_GROUPS,), jnp.int32)],
        compiler_params=pltpu.CompilerParams(dimension_semantics=("arbitrary", "arbitrary"),
                                             vmem_limit_bytes=VMEM_LIMIT),
        name="moe",
    )(x, xb, wr, rb, su, wgu, wd, g, b)


def _block_diag(w):
    g, n, _ = w.shape
    eye = jnp.eye(g, dtype=w.dtype)
    return (eye[:, None, :, None] * w[:, :, None, :]).reshape(g * n, g * n)


def _per_head(w, off, n_heads, width, parts):
    cols = [w[:, off + p * n_heads * width: off + (p + 1) * n_heads * width]
            .reshape(w.shape[0], n_heads, width) for p in parts]
    return jnp.transpose(jnp.concatenate(cols, axis=2), (1, 0, 2))


def _row(v):
    return v.reshape(1, -1).astype(F32)


def kernel(x, w_in, conv_a_w, conv_a_b, rg_w_a, rg_b_a, rg_w_x, rg_b_x, rg_lambda, gdn_conv_w, gdn_a_log, gdn_dt_bias, gdn_norm_w, gla_w_gate_up, gla_b_gate, gla_norm_w, w_branch, w_out, ln1_g, ln1_b, w_router, router_bias, w_gate, w_up, w_down, ln2_g, ln2_b):
    bsz, s, d = x.shape
    t = bsz * s
    c = GDN_CHUNK
    ii = jnp.arange(SB_BLOCK)
    u2 = jnp.concatenate([(ii[:, None] >= ii[None, :]).astype(BF16),
                          jnp.ones((SB_BLOCK, SB_BLOCK), BF16)], axis=1)
    u2 = jnp.concatenate([u2, u2], axis=0)
    jj = jnp.arange(c)
    tri = (jj[:, None] >= jj[None, :]).astype(BF16)
    wr = w_router.astype(F32).T
    rb = router_bias.astype(F32).reshape(N_EXPERTS, 1)
    kk = jnp.arange(MOE_TILE)
    su = (kk[:, None] < kk[None, :]).astype(BF16)

    xf = x.reshape(t, d)
    xb = xf.astype(BF16)
    for l in range(DEPTH):
        wl = w_in[l]
        wlb = wl.astype(BF16)
        xb3 = xb.reshape(bsz, s, d)

        wg = jnp.concatenate([_block_diag(rg_w_a[l]), _block_diag(rg_w_x[l])], axis=1).astype(BF16)
        bg = jnp.concatenate([rg_b_a[l], rg_b_x[l]]).reshape(1, -1).astype(F32)
        ya = _lru_call(xb3, wlb[:, _OFF_A:_OFF_A + LRU_WIDTH], conv_a_w[l].astype(F32),
                       _row(conv_a_b[l]), wg, bg, _row(rg_lambda[l]))

        w_gdn = jnp.concatenate([_per_head(wlb, _OFF_GDN_QKV, GDN_HEADS, GDN_DK, (0, 1, 2)),
                                 _per_head(wlb, _OFF_GDN_GATE, GDN_HEADS, GDN_DV, (0,))], axis=2)
        wbd = jnp.stack([wlb[:, _OFF_GDN_BETA:_OFF_GDN_BETA + GDN_HEADS],
                         wlb[:, _OFF_GDN_DECAY:_OFF_GDN_DECAY + GDN_HEADS]], axis=2)
        wbd = jnp.pad(jnp.transpose(wbd, (1, 0, 2)), ((0, 0), (0, 0), (0, LANES - 2)))
        cw_gdn = jnp.transpose(gdn_conv_w[l].astype(F32).reshape(CONV_WIDTH, 3, GDN_HEADS, GDN_DK),
                               (2, 0, 1, 3)).reshape(GDN_HEADS, CONV_WIDTH, 3 * GDN_DK)
        hp = jnp.zeros((GDN_HEADS, SUBLANES, GDN_DK), F32)
        hp = hp.at[:, 0, :].set(gdn_a_log[l].astype(F32)[:, None])
        hp = hp.at[:, 1, :].set(gdn_dt_bias[l].astype(F32)[:, None])
        yb = _gdn_call(xb3, w_gdn, wbd, cw_gdn, hp, _row(gdn_norm_w[l]), tri)

        yc = _sb_call(xb3, wlb[:, _OFF_SB:_OFF_SB + 3 * SB_HEADS * SB_DH], u2)

        q_off = _OFF_GLA_QKV
        w_gla = jnp.concatenate([
            _per_head(wlb, q_off, GLA_HEADS, GLA_DK, (0, 1)),
            _per_head(wlb, q_off + 2 * GLA_HEADS * GLA_DK, GLA_HEADS, GLA_DV, (0,)),
            _per_head(wlb, _OFF_GLA_GATE, GLA_HEADS, GLA_DV, (0,))], axis=2)
        wlr = jnp.pad(wlb[:, _OFF_GLA_LR:_OFF_GLA_LR + GLA_GATE_RANK],
                      ((0, 0), (0, LANES - GLA_GATE_RANK)))
        wgu = jnp.pad(jnp.transpose(gla_w_gate_up[l].reshape(GLA_GATE_RANK, GLA_HEADS, GLA_DK), (1, 0, 2)),
                      ((0, 0), (0, LANES - GLA_GATE_RANK), (0, 0))).astype(BF16)
        bgl = gla_b_gate[l].astype(F32).reshape(GLA_HEADS, 1, GLA_DK)
        yd = _gla_call(xb3, w_gla, wlr, wgu, bgl, _row(gla_norm_w[l]), tri)

        xf, xb = _merge_call(xf, xb, ya.reshape(t, -1), yb.reshape(t, -1), yc.reshape(t, -1),
                             yd.reshape(t, -1), wlb[:, _OFF_MERGE:], w_branch[l].astype(BF16),
                             w_out[l].astype(BF16), _row(ln1_g[l]), _row(ln1_b[l]))

        w_gu = jnp.concatenate([w_gate[l].astype(BF16), w_up[l].astype(BF16)], axis=2)
        w_gu = w_gu.reshape(N_GROUPS, EXPERTS_PER_GROUP, d, 2 * D_EXPERT)
        w_dn = w_down[l].astype(BF16).reshape(N_GROUPS, EXPERTS_PER_GROUP, D_EXPERT, d)
        xf, xb = _moe_call(xf, xb, wr, rb, su, w_gu, w_dn, _row(ln2_g[l]), _row(ln2_b[l]))
    return xf.reshape(bsz, s, d)
```
